```python
import math
import jax, jax.numpy as jnp
from jax import lax
import numpy as np

D_MODEL = 1024
BATCH = 1
SEQ = 16384
DEPTH = 2
DEC_BATCH = 8
DEC_SEQ = 4096
PAST_LEN = 128

HEAD_DIM = 64
FNET_WIDTH = D_MODEL // 2
FNET_GROUPS = FNET_WIDTH // HEAD_DIM
RWKV_WIDTH = D_MODEL - FNET_WIDTH
RWKV_HEADS = RWKV_WIDTH // HEAD_DIM
DECAY_LORA = 64
ICLR_LORA = 64
GATE_LORA = 128
R_OFF = 0
K_OFF = RWKV_WIDTH
V_OFF = 2 * RWKV_WIDTH
WD_OFF = 3 * RWKV_WIDTH
AD_OFF = WD_OFF + 2 * DECAY_LORA
GD_OFF = AD_OFF + 2 * ICLR_LORA
RWKV_IN = GD_OFF + GATE_LORA
MIX0_IN = FNET_WIDTH + RWKV_IN
GN_EPS = 64e-5
DIFF_HEADS = D_MODEL // (2 * HEAD_DIM)
DIFF_QK_DIM = HEAD_DIM
DIFF_V_DIM = 2 * HEAD_DIM
DIFF_Q_WIDTH = DIFF_HEADS * 2 * DIFF_QK_DIM
DIFF_V_WIDTH = DIFF_HEADS * DIFF_V_DIM
ROPE_THETA = 10000.0
Q_BLOCK = 128
SUBLN_EPS = 1e-5
D_FF = (11 * D_MODEL) // 4
CONV_W = 3
NORM_EPS = 1e-6
N_EVEN = (DEPTH + 1) // 2
N_ODD = DEPTH // 2

kernel_name = "fnet_rwkv7_diffattn_convffn_encoder"


def rms_norm(x, g, eps=NORM_EPS):
    xf = x.astype(jnp.float32)
    y = xf * lax.rsqrt(jnp.mean(xf * xf, axis=-1, keepdims=True) + eps)
    return (y * g.astype(jnp.float32)).astype(x.dtype)


def shift_prev(z):
    return jnp.pad(z[:, :-1], ((0, 0), (1, 0), (0, 0)))


def shift_next(z):
    return jnp.pad(z[:, 1:], ((0, 0), (0, 1), (0, 0)))


def rotary(x):
    s_len, dim = x.shape[1], x.shape[-1]
    half = dim // 2
    inv_freq = ROPE_THETA ** (-jnp.arange(half, dtype=jnp.float32) / half)
    ang = jnp.arange(s_len, dtype=jnp.float32)[:, None] * inv_freq[None, :]
    bshape = (1, s_len) + (1,) * (x.ndim - 3) + (half,)
    cos = jnp.cos(ang).reshape(bshape)
    sin = jnp.sin(ang).reshape(bshape)
    xf = x.astype(jnp.float32)
    x1, x2 = xf[..., :half], xf[..., half:]
    return jnp.concatenate([x1 * cos - x2 * sin, x2 * cos + x1 * sin], axis=-1).astype(x.dtype)


def to_heads(t):
    return t.reshape(t.shape[:-1] + (RWKV_HEADS, HEAD_DIM))


def wkv7_scan(r, decay, k, v, a, b):
    bsz, _, nh, n = r.shape

    def step(state, inp):
        r_t, w_t, k_t, v_t, a_t, b_t = inp
        sa = jnp.einsum('bhvk,bhk->bhv', state, a_t)
        state = (state * w_t[:, :, None, :] + sa[..., None] * b_t[:, :, None, :]
                 + v_t[..., None] * k_t[:, :, None, :])
        y = jnp.einsum('bhvk,bhk->bhv', state, r_t)
        return state, y

    xs = tuple(jnp.swapaxes(t, 0, 1) for t in (r, decay, k, v, a, b))
    s0 = jnp.zeros((bsz, nh, n, n), jnp.float32)
    _, y = lax.scan(step, s0, xs)
    return jnp.swapaxes(y, 0, 1)


def rwkv7_bidirectional(z, w0, w2, a0, a2, g2, k_k, k_a, r_k, lnx_g, lnx_b):
    bsz, s_len, _ = z.shape
    f32 = jnp.float32
    zf = z.astype(f32)
    r = zf[..., R_OFF:R_OFF + RWKV_WIDTH]
    k = zf[..., K_OFF:K_OFF + RWKV_WIDTH]
    v = zf[..., V_OFF:V_OFF + RWKV_WIDTH]
    wd = zf[..., WD_OFF:AD_OFF].reshape(bsz, s_len, 2, DECAY_LORA)
    ad = zf[..., AD_OFF:GD_OFF].reshape(bsz, s_len, 2, ICLR_LORA)
    gd = zf[..., GD_OFF:RWKV_IN]
    w_log = -jax.nn.softplus(-(w0.astype(f32) + jnp.einsum('bsdr,drc->bsdc', jnp.tanh(wd), w2.astype(f32)))) - 0.5
    decay = jnp.exp(-jnp.exp(w_log))
    iclr = jax.nn.sigmoid(a0.astype(f32) + jnp.einsum('bsdr,drc->bsdc', ad, a2.astype(f32)))
    gate = jax.nn.sigmoid(gd) @ g2.astype(f32)
    kk = to_heads(k * k_k.astype(f32))
    kk = kk / jnp.maximum(jnp.sqrt(jnp.sum(kk * kk, axis=-1, keepdims=True)), 1e-12)
    r_h, v_h = to_heads(r), to_heads(v)
    r_kf = r_k.astype(f32)
    y = jnp.zeros_like(r_h)
    bonus = jnp.zeros((bsz, s_len, RWKV_HEADS, 1), f32)
    for d in range(2):
        a_d = to_heads(iclr[:, :, d])
        k_d = to_heads(k * (1.0 + (iclr[:, :, d] - 1.0) * k_a.astype(f32)))
        ins = (r_h, to_heads(decay[:, :, d]), k_d, v_h, -kk, kk * a_d)
        if d == 1:
            ins = tuple(t[:, ::-1] for t in ins)
        y_d = wkv7_scan(*ins)
        if d == 1:
            y_d = y_d[:, ::-1]
        y = y + y_d
        bonus = bonus + jnp.sum(r_h * k_d * r_kf, axis=-1, keepdims=True)
    mu = jnp.mean(y, axis=-1, keepdims=True)
    var = jnp.mean(jnp.square(y - mu), axis=-1, keepdims=True)
    yn = ((y - mu) * lax.rsqrt(var + GN_EPS)).reshape(bsz, s_len, RWKV_WIDTH)
    yn = yn * lnx_g.astype(f32) + lnx_b.astype(f32)
    out = (yn + (bonus * v_h).reshape(bsz, s_len, RWKV_WIDTH)) * gate
    return out.astype(z.dtype)


def fourier_rwkv_mixer(h, w_in, mu_prev, mu_next, w0, w2, a0, a2, g2, k_k, k_a, r_k, lnx_g, lnx_b, w_out):
    bsz, s_len, _ = h.shape
    z = h @ w_in
    u = z[..., :FNET_WIDTH]
    zr = z[..., FNET_WIDTH:]
    zr = zr + mu_prev * (shift_prev(zr) - zr) + mu_next * (shift_next(zr) - zr)
    ug = u.astype(jnp.float32).reshape(bsz, s_len, FNET_GROUPS, HEAD_DIM)
    f = jnp.fft.fft2(ug, axes=(1, 3), norm='ortho').real
    f = f.reshape(bsz, s_len, FNET_WIDTH).astype(h.dtype)
    y_rwkv = rwkv7_bidirectional(zr, w0, w2, a0, a2, g2, k_k, k_a, r_k, lnx_g, lnx_b)
    return jnp.concatenate([f, y_rwkv], axis=-1) @ w_out


def diff_attention(h, w_qkv, lq1, lk1, lq2, lk2, subln_g, w_o, lambda_init):
    bsz, s_len, _ = h.shape
    f32 = jnp.float32
    qkv = h @ w_qkv
    q = qkv[..., :DIFF_Q_WIDTH].reshape(bsz, s_len, DIFF_HEADS, 2, DIFF_QK_DIM)
    k = qkv[..., DIFF_Q_WIDTH:2 * DIFF_Q_WIDTH].reshape(bsz, s_len, DIFF_HEADS, 2, DIFF_QK_DIM)
    v = qkv[..., 2 * DIFF_Q_WIDTH:].reshape(bsz, s_len, DIFF_HEADS, DIFF_V_DIM)
    q = rotary(q) * (DIFF_QK_DIM ** -0.5)
    k = rotary(k)
    lam = (jnp.exp(jnp.sum(lq1.astype(f32) * lk1.astype(f32)))
           - jnp.exp(jnp.sum(lq2.astype(f32) * lk2.astype(f32))) + lambda_init)
    n_blk = s_len // Q_BLOCK
    qb = jnp.moveaxis(q.reshape(bsz, n_blk, Q_BLOCK, DIFF_HEADS, 2, DIFF_QK_DIM), 1, 0)

    def block(q_blk):
        s = jnp.einsum('bqhcd,bkhcd->bhcqk', q_blk, k).astype(f32)
        p = jax.nn.softmax(s, axis=-1)
        attn = (p[:, :, 0] - lam * p[:, :, 1]).astype(v.dtype)
        return jnp.einsum('bhqk,bkhe->bqhe', attn, v)

    o = lax.map(block, qb)
    o = jnp.moveaxis(o, 0, 1).reshape(bsz, s_len, DIFF_HEADS, DIFF_V_DIM)
    o = rms_norm(o, subln_g, SUBLN_EPS) * (1.0 - lambda_init)
    return o.reshape(bsz, s_len, DIFF_V_WIDTH) @ w_o


def conv_ffn(h, w_up, conv_w, conv_b, w_down):
    u = h @ w_up
    u = conv_w[0] * shift_prev(u) + conv_w[1] * u + conv_w[2] * shift_next(u) + conv_b
    val, gate = u[..., :D_FF], u[..., D_FF:]
    return (jax.nn.gelu(gate, approximate=True) * val) @ w_down


def setup_inputs(seed: int = 0) -> dict:
    key = jax.random.key(seed)
    ks = iter(jax.random.split(key, 48))
    f32 = jnp.float32

    def nrm(shape, scale):
        return scale * jax.random.normal(next(ks), shape, f32)

    def gain(shape):
        return 1.0 + 0.05 * jax.random.normal(next(ks), shape, f32)

    return {
        'x_prompt': nrm((BATCH, SEQ, D_MODEL), 1.0),
        'x_sample': nrm((DEC_BATCH, DEC_SEQ, D_MODEL), 1.0),
        'mix0_norm_pre': gain((N_EVEN, D_MODEL)),
        'mix0_norm_post': gain((N_EVEN, D_MODEL)),
        'w_in0': nrm((N_EVEN, D_MODEL, MIX0_IN), D_MODEL ** -0.5),
        'mu_prev': jax.random.uniform(next(ks), (N_EVEN, RWKV_IN), f32, 0.0, 0.5),
        'mu_next': jax.random.uniform(next(ks), (N_EVEN, RWKV_IN), f32, 0.0, 0.5),
        'decay_w0': jax.random.uniform(next(ks), (N_EVEN, 2, RWKV_WIDTH), f32, -5.0, 1.0),
        'decay_w2': nrm((N_EVEN, 2, DECAY_LORA, RWKV_WIDTH), 0.5 * DECAY_LORA ** -0.5),
        'iclr_a0': nrm((N_EVEN, 2, RWKV_WIDTH), 0.1),
        'iclr_a2': nrm((N_EVEN, 2, ICLR_LORA, RWKV_WIDTH), 0.5 * ICLR_LORA ** -0.5),
        'gate_g2': nrm((N_EVEN, GATE_LORA, RWKV_WIDTH), GATE_LORA ** -0.5),
        'k_k': 0.85 + nrm((N_EVEN, RWKV_WIDTH), 0.05),
        'k_a': 1.0 + nrm((N_EVEN, RWKV_WIDTH), 0.05),
        'r_k': nrm((N_EVEN, RWKV_HEADS, HEAD_DIM), 0.1),
        'lnx_g': gain((N_EVEN, RWKV_WIDTH)),
        'lnx_b': nrm((N_EVEN, RWKV_WIDTH), 0.01),
        'w_out0': nrm((N_EVEN, D_MODEL, D_MODEL), D_MODEL ** -0.5),
        'mix1_norm_pre': gain((N_ODD, D_MODEL)),
        'mix1_norm_post': gain((N_ODD, D_MODEL)),
        'w_qkv1': nrm((N_ODD, D_MODEL, 2 * DIFF_Q_WIDTH + DIFF_V_WIDTH), D_MODEL ** -0.5),
        'lambda_q1': nrm((N_ODD, DIFF_QK_DIM), 0.1),
        'lambda_k1': nrm((N_ODD, DIFF_QK_DIM), 0.1),
        'lambda_q2': nrm((N_ODD, DIFF_QK_DIM), 0.1),
        'lambda_k2': nrm((N_ODD, DIFF_QK_DIM), 0.1),
        'subln_g': gain((N_ODD, DIFF_V_DIM)),
        'w_o1': nrm((N_ODD, DIFF_V_WIDTH, D_MODEL), DIFF_V_WIDTH ** -0.5),
        'ffn_norm_pre': gain((DEPTH, D_MODEL)),
        'ffn_norm_post': gain((DEPTH, D_MODEL)),
        'w_up': nrm((DEPTH, D_MODEL, 2 * D_FF), D_MODEL ** -0.5),
        'conv_w': nrm((DEPTH, CONV_W, 2 * D_FF), CONV_W ** -0.5),
        'conv_b': nrm((DEPTH, 2 * D_FF), 0.01),
        'w_down': nrm((DEPTH, D_FF, D_MODEL), D_FF ** -0.5),
    }


def reference(x_prompt, x_sample,
              mix0_norm_pre, mix0_norm_post, w_in0, mu_prev, mu_next, decay_w0, decay_w2,
              iclr_a0, iclr_a2, gate_g2, k_k, k_a, r_k, lnx_g, lnx_b, w_out0,
              mix1_norm_pre, mix1_norm_post, w_qkv1, lambda_q1, lambda_k1, lambda_q2, lambda_k2,
              subln_g, w_o1,
              ffn_norm_pre, ffn_norm_post, w_up, conv_w, conv_b, w_down):
    def trunk(x):
        for layer in range(DEPTH):
            if layer % 2 == 0:
                i = layer // 2
                h = rms_norm(x, mix0_norm_pre[i])
                h = fourier_rwkv_mixer(h, w_in0[i], mu_prev[i], mu_next[i], decay_w0[i], decay_w2[i],
                                       iclr_a0[i], iclr_a2[i], gate_g2[i], k_k[i], k_a[i], r_k[i],
                                       lnx_g[i], lnx_b[i], w_out0[i])
                x = x + rms_norm(h, mix0_norm_post[i])
            else:
                i = layer // 2
                lambda_init = 0.8 - 0.6 * math.exp(-0.3 * layer)
                h = rms_norm(x, mix1_norm_pre[i])
                h = diff_attention(h, w_qkv1[i], lambda_q1[i], lambda_k1[i], lambda_q2[i], lambda_k2[i],
                                   subln_g[i], w_o1[i], lambda_init)
                x = x + rms_norm(h, mix1_norm_post[i])
            h = rms_norm(x, ffn_norm_pre[layer])
            h = conv_ffn(h, w_up[layer], conv_w[layer], conv_b[layer], w_down[layer])
            x = x + rms_norm(h, ffn_norm_post[layer])
        return x

    y_prompt = trunk(x_prompt)
    y_sample = trunk(x_sample)
    return (y_prompt, y_sample)
```

```python
import functools
import math

import numpy as np
import jax
import jax.numpy as jnp
from jax import lax
from jax.experimental import pallas as pl
from jax.experimental.pallas import tpu as pltpu

F32 = jnp.float32
BF16 = jnp.bfloat16

D_MODEL = 1024
HEAD_DIM = 64
FNET_WIDTH = 512
RWKV_WIDTH = 512
RWKV_HEADS = 8
RWKV_IN = 1920
D_FF = 2816
NORM_EPS = 1e-6
GN_EPS = 64e-5
SUBLN_EPS = 1e-5
ROPE_THETA = 10000.0
DIFF_HEADS = 8
LANES = 128
HALO = 8
CHUNK = 64
VMEM_LIMIT = 56 * 1024 * 1024
HIGHEST = lax.Precision.HIGHEST


def _params(*sem):
    return pltpu.CompilerParams(dimension_semantics=sem, vmem_limit_bytes=VMEM_LIMIT)


def _rms(x, g, eps):
    return x * lax.rsqrt(jnp.mean(x * x, axis=-1, keepdims=True) + eps) * g


def _dot(a, b):
    return jnp.dot(a, b, preferred_element_type=F32)


def _dot_nt(a, b):
    return lax.dot_general(a, b, (((1,), (1,)), ((), ())), preferred_element_type=F32)


def _dot_tn(a, b):
    return lax.dot_general(a, b, (((0,), (0,)), ((), ())), preferred_element_type=F32)


def _sigmoid(x):
    return 1.0 / (1.0 + jnp.exp(-x))


def _segsum(x, ones_bd):
    hi = x.astype(BF16)
    lo = (x - hi.astype(F32)).astype(BF16)
    return _dot(hi, ones_bd) + _dot(lo, ones_bd)


def _in_proj_body(x_ref, g_ref, w_ref, u_ref, z_ref):
    h = _rms(x_ref[...], g_ref[...], NORM_EPS).astype(BF16)
    y = _dot(h, w_ref[...])
    u_ref[...] = y[:, :FNET_WIDTH].astype(BF16)
    z_ref[...] = y[:, FNET_WIDTH:]


def _in_proj(x2d, g, w, tm=512):
    t, d = x2d.shape
    n = w.shape[1]
    return pl.pallas_call(
        _in_proj_body,
        out_shape=(jax.ShapeDtypeStruct((t, FNET_WIDTH), BF16),
                   jax.ShapeDtypeStruct((t, n - FNET_WIDTH), F32)),
        grid=(t // tm,),
        in_specs=[pl.BlockSpec((tm, d), lambda i: (i, 0)),
                  pl.BlockSpec((1, d), lambda i: (0, 0)),
                  pl.BlockSpec((d, n), lambda i: (0, 0))],
        out_specs=(pl.BlockSpec((tm, FNET_WIDTH), lambda i: (i, 0)),
                   pl.BlockSpec((tm, n - FNET_WIDTH), lambda i: (i, 0))),
        compiler_params=_params("parallel"),
        name="in_proj",
    )(x2d, g.reshape(1, d), w)


def _rwkv_prep_body(z_ref, zp_ref, zn_ref, mup_ref, mun_ref, w0_ref, w2_ref, a0_ref, a2_ref, g2_ref,
                    kk_ref, ka_ref, rk_ref, ones_ref,
                    r_o, v_o, kn_o, lwf_o, lwb_o, kf_o, kb_o, bf_o, bb_o, gate_o, bonus_o):
    i = pl.program_id(1)
    n = pl.num_programs(1)
    z = z_ref[...]
    tm = z.shape[0]
    rows = lax.broadcasted_iota(jnp.int32, z.shape, 0)
    prev_row = jnp.where(i == 0, 0.0, zp_ref[HALO - 1:HALO, :])
    next_row = jnp.where(i == n - 1, 0.0, zn_ref[0:1, :])
    z_prev = jnp.where(rows == 0, prev_row, pltpu.roll(z, 1, 0))
    z_next = jnp.where(rows == tm - 1, next_row, pltpu.roll(z, tm - 1, 0))
    zs = z + mup_ref[...] * (z_prev - z) + mun_ref[...] * (z_next - z)

    w = RWKV_WIDTH
    r = zs[:, 0:w]
    k = zs[:, w:2 * w]
    v = zs[:, 2 * w:3 * w]
    wd = zs[:, 3 * w:3 * w + LANES]
    ad = zs[:, 3 * w + LANES:3 * w + 2 * LANES]
    gd = zs[:, 3 * w + 2 * LANES:3 * w + 3 * LANES]

    yw = w0_ref[...] + _dot(jnp.tanh(wd).astype(BF16), w2_ref[...])
    lw = -math.exp(-0.5) * _sigmoid(yw)
    iclr = _sigmoid(a0_ref[...] + _dot(ad.astype(BF16), a2_ref[...]))
    gate = _dot(_sigmoid(gd).astype(BF16), g2_ref[...])

    ones_bd = ones_ref[...]
    kk = k * kk_ref[...]
    kn = kk / jnp.maximum(jnp.sqrt(_segsum(kk * kk, ones_bd)), 1e-12)
    ka = ka_ref[...]
    k_f = k * (1.0 + (iclr[:, :w] - 1.0) * ka)
    k_b = k * (1.0 + (iclr[:, w:] - 1.0) * ka)
    bonus = _segsum(r * rk_ref[...] * (k_f + k_b), ones_bd)

    r_o[...] = r
    v_o[...] = v
    kn_o[...] = kn
    lwf_o[...] = lw[:, :w]
    lwb_o[...] = lw[:, w:]
    kf_o[...] = k_f
    kb_o[...] = k_b
    bf_o[...] = kn * iclr[:, :w]
    bb_o[...] = kn * iclr[:, w:]
    gate_o[...] = gate
    bonus_o[...] = bonus * v


def _rwkv_prep(z, bsz, s_len, consts, tm=256):
    t, n = z.shape
    nb = s_len // tm
    nh = t // HALO
    w = RWKV_WIDTH

    def full(a):
        return pl.BlockSpec(a.shape, lambda b, i: (0,) * a.ndim)

    out = jax.ShapeDtypeStruct((t, w), F32)
    return pl.pallas_call(
        _rwkv_prep_body,
        out_shape=(out,) * 11,
        grid=(bsz, nb),
        in_specs=[pl.BlockSpec((tm, n), lambda b, i: (b * nb + i, 0)),
                  pl.BlockSpec((HALO, n), lambda b, i: (jnp.maximum((b * nb + i) * (tm // HALO) - 1, 0), 0)),
                  pl.BlockSpec((HALO, n), lambda b, i: (jnp.minimum((b * nb + i + 1) * (tm // HALO), nh - 1), 0)),
                  ] + [full(a) for a in consts],
        out_specs=(pl.BlockSpec((tm, w), lambda b, i: (b * nb + i, 0)),) * 11,
        compiler_params=_params("parallel", "parallel"),
        name="rwkv_prep",
    )(z, z, z, *consts)


def _wkv_direction(reverse, r_ref, v_ref, kn_ref, lw_ref, k_ref, b_ref, tri_ref, y_ref, h_ref):
    c = CHUNK
    row = lax.broadcasted_iota(jnp.int32, (c, LANES), 0)
    lane = lax.broadcasted_iota(jnp.int32, (c, LANES), 1)
    col = lane % c
    if reverse:
        strict, incl = row < col, row <= col
    else:
        strict, incl = row > col, row >= col
    r128 = lax.broadcasted_iota(jnp.int32, (LANES, LANES), 0)
    c128 = lax.broadcasted_iota(jnp.int32, (LANES, LANES), 1)
    same_head = (r128 // HEAD_DIM) == (c128 // HEAD_DIM)
    zeros = jnp.zeros((c, LANES), F32)
    ones = jnp.ones((c, LANES), F32)

    lw_all = lw_ref[...]
    cum_all = jnp.dot(tri_ref[...], lw_all, precision=HIGHEST, preferred_element_type=F32)
    for p in range(RWKV_WIDTH // LANES):
        sl = slice(p * LANES, (p + 1) * LANES)
        lw = lw_all[:, sl]
        cum = cum_all[:, sl]
        tot = jnp.sum(lw, axis=0, keepdims=True)
        r, v, kn, k, b = r_ref[:, sl], v_ref[:, sl], kn_ref[:, sl], k_ref[:, sl], b_ref[:, sl]
        e_neg = jnp.exp(-cum)
        e_end = jnp.exp(tot - cum)
        rt = r * jnp.exp(cum)
        at = -kn * jnp.exp(cum - lw)
        bt = b * e_neg
        kt = k * e_neg
        bbar = b * e_end
        kbar = k * e_end
        rhs = jnp.concatenate([bt, kt], axis=0).astype(BF16)
        p1 = zeros
        p2 = zeros
        q1 = zeros
        q2 = zeros
        for e in range(LANES // HEAD_DIM):
            m = (lane // HEAD_DIM) == e
            am = jnp.where(m, at, 0.0)
            rm = jnp.where(m, rt, 0.0)
            vm = jnp.where(m, v, 0.0)
            aa = _dot_nt(jnp.concatenate([am, rm], axis=0).astype(BF16), rhs)
            top = jnp.where(strict, aa[:c], 0.0)
            bot = jnp.where(incl, aa[c:], 0.0)
            akv = _dot(top.astype(BF16), jnp.concatenate([zeros, vm], axis=0).astype(BF16))
            x = jnp.concatenate([am, akv], axis=1)
            pw = top[:, :c]
            n_lvl = int(math.log2(c))
            for lvl in range(n_lvl):
                if lvl < n_lvl - 1:
                    pr = _dot(pw.astype(BF16), jnp.concatenate([x, pw], axis=1).astype(BF16))
                    x = x + pr[:, :2 * LANES]
                    pw = pr[:, 2 * LANES:]
                else:
                    x = x + _dot(pw.astype(BF16), x.astype(BF16))
            p1e = x[:, :LANES]
            p2e = x[:, LANES:]
            q1 = q1 + rm + _dot(bot[:, :c].astype(BF16), p1e.astype(BF16))
            q2 = q2 + _dot(bot.astype(BF16), jnp.concatenate([p2e, vm], axis=0).astype(BF16))
            p1 = p1 + p1e
            p2 = p2 + p2e
        bp = jnp.where(same_head, _dot_tn(bbar.astype(BF16), p1.astype(BF16)), 0.0)
        g = jnp.where(same_head,
                      _dot_tn(jnp.concatenate([bbar, kbar], axis=0).astype(BF16),
                              jnp.concatenate([p2, v], axis=0).astype(BF16)), 0.0)
        tot_t = lax.dot_general(lw, ones, (((0,), (0,)), ((), ())), precision=HIGHEST,
                                preferred_element_type=F32)
        h = h_ref[p]
        hb = h.astype(BF16)
        y_ref[:, sl] = _dot(q1.astype(BF16), hb) + q2
        h_ref[p] = jnp.exp(tot_t) * h + _dot(bp.astype(BF16), hb) + g


def _wkv_body(rf, vf, knf, lwf, kf, bf, rb, vb, knb, lwb, kb, bb, trif, trib, yf, yb, hf_ref, hb_ref):
    @pl.when(pl.program_id(1) == 0)
    def _():
        hf_ref[...] = jnp.zeros_like(hf_ref)
        hb_ref[...] = jnp.zeros_like(hb_ref)

    _wkv_direction(False, rf, vf, knf, lwf, kf, bf, trif, yf, hf_ref)
    _wkv_direction(True, rb, vb, knb, lwb, kb, bb, trib, yb, hb_ref)


def _wkv_scan(r, v, kn, lwf, lwb, kf, kb, bf, bb, bsz, s_len):
    t, w = r.shape
    c = CHUNK
    nc = s_len // c
    tri = np.tril(np.ones((c, c), np.float32))
    fwd = pl.BlockSpec((c, w), lambda b, i: (b * nc + i, 0))
    bwd = pl.BlockSpec((c, w), lambda b, i: (b * nc + nc - 1 - i, 0))
    cst = pl.BlockSpec((c, c), lambda b, i: (0, 0))
    out = jax.ShapeDtypeStruct((t, w), F32)
    n_pairs = w // LANES
    return pl.pallas_call(
        _wkv_body,
        out_shape=(out, out),
        grid=(bsz, nc),
        in_specs=[fwd] * 6 + [bwd] * 6 + [cst, cst],
        out_specs=(fwd, bwd),
        scratch_shapes=[pltpu.VMEM((n_pairs, LANES, LANES), F32),
                        pltpu.VMEM((n_pairs, LANES, LANES), F32)],
        compiler_params=_params("parallel", "arbitrary"),
        name="wkv_scan",
    )(r, v, kn, lwf, kf, bf, r, v, kn, lwb, kb, bb, jnp.asarray(tri), jnp.asarray(tri.T))


def _dft_split(s_len):
    s1 = 1 << ((s_len.bit_length() - 1 + 1) // 2)
    return s1, s_len // s1


def _dft_tables(s_len):
    s1, s2 = _dft_split(s_len)

    def cs(num, period):
        ang = (num % period).astype(F32) * (2.0 * math.pi / period)
        return jnp.cos(ang), jnp.sin(ang)

    k1 = jnp.arange(s1, dtype=jnp.int32)
    c1, sn1 = cs(k1[:, None] * k1[None, :], s1)
    w1 = jnp.concatenate([c1, sn1], axis=0)
    k = k1[:, None, None] + s1 * jnp.arange(s2, dtype=jnp.int32)[None, :, None]
    ec, es = cs(k * jnp.arange(s2, dtype=jnp.int32)[None, None, :], s_len)
    e = jnp.concatenate([jnp.concatenate([ec, -es], axis=2),
                         jnp.concatenate([-es, -ec], axis=2)], axis=1)
    c = jnp.arange(HEAD_DIM, dtype=jnp.int32)
    c3, s3 = cs(c[:, None] * c[None, :], HEAD_DIM)
    eye = jnp.eye(2 * LANES // HEAD_DIM, dtype=F32)
    return (w1.astype(BF16), e.astype(BF16), jnp.kron(eye, c3).astype(BF16), jnp.kron(eye, s3).astype(BF16))


def _dft1_body(w_ref, u_ref, y_ref):
    y_ref[...] = _dot(w_ref[...], u_ref[...]).astype(BF16)


def _dft2_body(scale, e_ref, y_ref, c4_ref, s4_ref, f_ref):
    g = e_ref.shape[0]
    s2 = y_ref.shape[2]
    nw = 2 * LANES
    for j in range(g):
        yy = jnp.concatenate([y_ref[0, j], y_ref[1, j]], axis=0)
        x = _dot(e_ref[j], yy)
        xr = x[:s2].astype(BF16)
        xi = x[s2:].astype(BF16)
        for q in range(FNET_WIDTH // nw):
            f = _dot(xr[:, q * nw:(q + 1) * nw], c4_ref[...]) + _dot(xi[:, q * nw:(q + 1) * nw], s4_ref[...])
            f_ref[:, j * FNET_WIDTH + q * nw:j * FNET_WIDTH + (q + 1) * nw] = f * scale


def _fnet(u, bsz, s_len, tables):
    w1, e, c4, s4 = tables
    s1, s2 = _dft_split(s_len)
    fw = FNET_WIDTH
    ncol = s2 * fw
    nb = min(ncol, 4096)
    u3 = u.reshape(bsz, s1, ncol)
    y = pl.pallas_call(
        _dft1_body,
        out_shape=jax.ShapeDtypeStruct((bsz, 2 * s1, ncol), BF16),
        grid=(bsz, ncol // nb),
        in_specs=[pl.BlockSpec((2 * s1, s1), lambda b, j: (0, 0)),
                  pl.BlockSpec((None, s1, nb), lambda b, j: (b, 0, j))],
        out_specs=pl.BlockSpec((None, 2 * s1, nb), lambda b, j: (b, 0, j)),
        compiler_params=_params("parallel", "parallel"),
        name="dft_stage1",
    )(w1, u3)
    y5 = y.reshape(bsz, 2, s1, s2, fw)
    g = 8
    scale = 1.0 / math.sqrt(s_len * HEAD_DIM)
    f = pl.pallas_call(
        functools.partial(_dft2_body, scale),
        out_shape=jax.ShapeDtypeStruct((bsz, s2, s1 * fw), F32),
        grid=(bsz, s1 // g),
        in_specs=[pl.BlockSpec((g, 2 * s2, 2 * s2), lambda b, j: (j, 0, 0)),
                  pl.BlockSpec((None, 2, g, s2, fw), lambda b, j: (b, 0, j, 0, 0)),
                  pl.BlockSpec(c4.shape, lambda b, j: (0, 0)),
                  pl.BlockSpec(s4.shape, lambda b, j: (0, 0))],
        out_specs=pl.BlockSpec((None, s2, g * fw), lambda b, j: (b, 0, j)),
        compiler_params=_params("parallel", "parallel"),
        name="dft_stage2",
    )(e, y5, c4, s4)
    return f.reshape(bsz * s_len, fw)


def _mix0_out_body(x_ref, f_ref, yf_ref, yb_ref, bonus_ref, gate_ref, lng_ref, lnb_ref, ones_ref, w_ref, g_ref,
                   o_ref):
    ones_bd = ones_ref[...]
    y = yf_ref[...] + yb_ref[...]
    mu = _segsum(y, ones_bd) * (1.0 / HEAD_DIM)
    d = y - mu
    var = _segsum(d * d, ones_bd) * (1.0 / HEAD_DIM)
    yn = d * lax.rsqrt(var + GN_EPS) * lng_ref[...] + lnb_ref[...]
    o = (yn + bonus_ref[...]) * gate_ref[...]
    cat = jnp.concatenate([f_ref[...].astype(BF16), o.astype(BF16)], axis=1)
    h = _dot(cat, w_ref[...])
    o_ref[...] = x_ref[...] + _rms(h, g_ref[...], NORM_EPS)


def _mix0_out(x2d, f, yf, yb, bonus, gate, lng, lnb, ones_bd, w_out, g_post, tm=512):
    t, d = x2d.shape
    w = RWKV_WIDTH
    row = lambda n: pl.BlockSpec((tm, n), lambda i: (i, 0))
    cst = lambda a: pl.BlockSpec(a.shape, lambda i: (0, 0))
    return pl.pallas_call(
        _mix0_out_body,
        out_shape=jax.ShapeDtypeStruct((t, d), F32),
        grid=(t // tm,),
        in_specs=[row(d), row(w), row(w), row(w), row(w), row(w),
                  cst(lng), cst(lnb), cst(ones_bd), cst(w_out), cst(g_post)],
        out_specs=row(d),
        compiler_params=_params("parallel"),
        name="mix0_out",
    )(x2d, f, yf, yb, bonus, gate, lng, lnb, ones_bd, w_out, g_post)


FF_CHUNK = 256


def _ffn_body(x_ref, xp_ref, xn_ref, gpre_ref, wup_ref, cw_ref, cb_ref, wdn_ref, gpost_ref, o_ref,
              h_ref, acc_ref):
    i = pl.program_id(1)
    n = pl.num_programs(1)
    tm = x_ref.shape[0]
    g = gpre_ref[...]
    x = x_ref[...]
    hp = jnp.where(i == 0, 0.0, _rms(xp_ref[...], g, NORM_EPS))
    hn = jnp.where(i == n - 1, 0.0, _rms(xn_ref[...], g, NORM_EPS))
    h_ref[...] = jnp.concatenate([hp, _rms(x, g, NORM_EPS), hn], axis=0).astype(BF16)
    acc_ref[...] = jnp.zeros_like(acc_ref)

    def chunk(j, carry):
        cv = pl.multiple_of(j * FF_CHUNK, FF_CHUNK)
        cg = pl.multiple_of(D_FF + j * FF_CHUNK, FF_CHUNK)
        h = h_ref[...]

        def conv(c0):
            u = _dot(h, wup_ref[:, pl.ds(c0, FF_CHUNK)])
            cw = cw_ref[:, pl.ds(c0, FF_CHUNK)]
            return (cw[0:1] * u[HALO - 1:HALO - 1 + tm] + cw[1:2] * u[HALO:HALO + tm]
                    + cw[2:3] * u[HALO + 1:HALO + 1 + tm] + cb_ref[:, pl.ds(c0, FF_CHUNK)])

        val = conv(cv)
        gate = conv(cg)
        act = 0.5 * gate * (1.0 + jnp.tanh(math.sqrt(2.0 / math.pi) * (gate + 0.044715 * gate * gate * gate)))
        acc_ref[...] += _dot((act * val).astype(BF16), wdn_ref[pl.ds(cv, FF_CHUNK), :])
        return carry

    lax.fori_loop(0, D_FF // FF_CHUNK, chunk, 0)
    o_ref[...] = x + _rms(acc_ref[...], gpost_ref[...], NORM_EPS)


def _ffn(x2d, bsz, s_len, g_pre, w_up, conv_w, conv_b, w_down, g_post, tm=512):
    t, d = x2d.shape
    nb = s_len // tm
    nh = t // HALO
    cst = lambda a: pl.BlockSpec(a.shape, lambda b, i: (0,) * a.ndim, pipeline_mode=pl.Buffered(1))
    return pl.pallas_call(
        _ffn_body,
        out_shape=jax.ShapeDtypeStruct((t, d), F32),
        grid=(bsz, nb),
        in_specs=[pl.BlockSpec((tm, d), lambda b, i: (b * nb + i, 0)),
                  pl.BlockSpec((HALO, d), lambda b, i: (jnp.maximum((b * nb + i) * (tm // HALO) - 1, 0), 0)),
                  pl.BlockSpec((HALO, d), lambda b, i: (jnp.minimum((b * nb + i + 1) * (tm // HALO), nh - 1), 0)),
                  cst(g_pre), cst(w_up), cst(conv_w), cst(conv_b), cst(w_down), cst(g_post)],
        out_specs=pl.BlockSpec((tm, d), lambda b, i: (b * nb + i, 0)),
        scratch_shapes=[pltpu.VMEM((tm + 2 * HALO, d), BF16), pltpu.VMEM((tm, d), F32)],
        compiler_params=_params("parallel", "parallel"),
        name="conv_ffn",
    )(x2d, x2d, x2d, g_pre, w_up, conv_w, conv_b, w_down, g_post)


def _qkv_body(x_ref, g_ref, wk_ref, wqt_ref, wvt_ref, ck_ref, sk_ref, cq_ref, sq_ref, k_o, qt_o, vt_o):
    h = _rms(x_ref[...], g_ref[...], NORM_EPS).astype(BF16)
    ck, sk = ck_ref[...], sk_ref[...]
    cq, sq = cq_ref[...], sq_ref[...]
    half = LANES // 2
    for hd in range(DIFF_HEADS):
        sl = slice(hd * LANES, (hd + 1) * LANES)
        kh = _dot(h, wk_ref[:, sl])
        k_o[:, sl] = (kh * ck + pltpu.roll(kh, half, 1) * sk).astype(BF16)
        qh = _dot_nt(wqt_ref[sl, :], h)
        qsw = jnp.concatenate([qh[half:], qh[:half]], axis=0)
        qt_o[sl, :] = ((qh * cq + qsw * sq) * (HEAD_DIM ** -0.5)).astype(BF16)
        vt_o[sl, :] = _dot_nt(wvt_ref[sl, :], h).astype(BF16)


def _qkv(x2d, bsz, s_len, g, wk, wqt, wvt, tabs, tm=512):
    t, d = x2d.shape
    nb = s_len // tm
    ck, sk, cq, sq = tabs
    cst = lambda a: pl.BlockSpec(a.shape, lambda b, i: (0, 0))
    tshape = jax.ShapeDtypeStruct((bsz, d, s_len), BF16)
    return pl.pallas_call(
        _qkv_body,
        out_shape=(jax.ShapeDtypeStruct((t, d), BF16), tshape, tshape),
        grid=(bsz, nb),
        in_specs=[pl.BlockSpec((tm, d), lambda b, i: (b * nb + i, 0)), cst(g), cst(wk), cst(wqt), cst(wvt),
                  pl.BlockSpec((tm, LANES), lambda b, i: (i, 0)),
                  pl.BlockSpec((tm, LANES), lambda b, i: (i, 0)),
                  pl.BlockSpec((LANES, tm), lambda b, i: (0, i)),
                  pl.BlockSpec((LANES, tm), lambda b, i: (0, i))],
        out_specs=(pl.BlockSpec((tm, d), lambda b, i: (b * nb + i, 0)),
                   pl.BlockSpec((None, d, tm), lambda b, i: (b, 0, i)),
                   pl.BlockSpec((None, d, tm), lambda b, i: (b, 0, i))),
        compiler_params=_params("parallel", "parallel"),
        name="qkv_rope",
    )(x2d, g, wk, wqt, wvt, ck, sk, cq, sq)


NEG_BIG = -1e30


def _attn_body(lambda_init, tk, qt_ref, k_ref, vt_ref, lq1, lk1, lq2, lk2, sg_ref, o_ref):
    qt = qt_ref[...]
    tq = qt.shape[1]
    s_len = k_ref.shape[0]
    rowi = lax.broadcasted_iota(jnp.int32, qt.shape, 0)
    first = (rowi // (HEAD_DIM // 2)) % 2 == 0
    zero = jnp.zeros_like(qt)
    q0 = jnp.where(first, qt, zero)
    q1 = jnp.where(first, zero, qt)

    def half_step(s, m, l, acc, vt):
        mn = jnp.maximum(m, jnp.max(s, axis=0, keepdims=True))
        al = jnp.exp(m - mn)
        p = jnp.exp(s - mn)
        l = al * l + jnp.sum(p, axis=0, keepdims=True)
        acc = al * acc + _dot(vt, p.astype(BF16))
        return mn, l, acc

    def step(j, carry):
        m0, l0, a0, m1, l1, a1 = carry
        off = pl.multiple_of(j * tk, tk)
        kk = k_ref[pl.ds(off, tk), :]
        vt = vt_ref[:, pl.ds(off, tk)]
        m0, l0, a0 = half_step(_dot(kk, q0), m0, l0, a0, vt)
        m1, l1, a1 = half_step(_dot(kk, q1), m1, l1, a1, vt)
        return m0, l0, a0, m1, l1, a1

    mi = jnp.full((1, tq), NEG_BIG, F32)
    li = jnp.zeros((1, tq), F32)
    ai = jnp.zeros((LANES, tq), F32)
    m0, l0, a0, m1, l1, a1 = lax.fori_loop(0, s_len // tk, step, (mi, li, ai, mi, li, ai))
    lam = (jnp.exp(jnp.sum(lq1[...] * lk1[...], axis=1, keepdims=True))
           - jnp.exp(jnp.sum(lq2[...] * lk2[...], axis=1, keepdims=True)) + lambda_init)
    o = a0 / l0 - lam * (a1 / l1)
    o = o * lax.rsqrt(jnp.mean(o * o, axis=0, keepdims=True) + SUBLN_EPS) * sg_ref[...] * (1.0 - lambda_init)
    o_ref[...] = o.astype(BF16)


def _attention(qt, k, vt, lq1, lk1, lq2, lk2, sg, lambda_init, bsz, s_len, tq=256, tk=512):
    d = qt.shape[1]
    k3 = k.reshape(bsz, s_len, d)
    cst = lambda a: pl.BlockSpec(a.shape, lambda b, h, i: (0, 0))
    return pl.pallas_call(
        functools.partial(_attn_body, lambda_init, tk),
        out_shape=jax.ShapeDtypeStruct((bsz, d, s_len), BF16),
        grid=(bsz, DIFF_HEADS, s_len // tq),
        in_specs=[pl.BlockSpec((None, LANES, tq), lambda b, h, i: (b, h, i)),
                  pl.BlockSpec((None, s_len, LANES), lambda b, h, i: (b, 0, h)),
                  pl.BlockSpec((None, LANES, s_len), lambda b, h, i: (b, h, 0)),
                  cst(lq1), cst(lk1), cst(lq2), cst(lk2), cst(sg)],
        out_specs=pl.BlockSpec((None, LANES, tq), lambda b, h, i: (b, h, i)),
        compiler_params=_params("parallel", "parallel", "parallel"),
        name="diff_attn",
    )(qt, k3, vt, lq1, lk1, lq2, lk2, sg)


def _attn_out_body(x_ref, ot_ref, w_ref, g_ref, o_ref):
    h = _dot_tn(ot_ref[...], w_ref[...])
    o_ref[...] = x_ref[...] + _rms(h, g_ref[...], NORM_EPS)


def _attn_out(x2d, ot, w_o, g_post, bsz, s_len, tm=512):
    t, d = x2d.shape
    nb = s_len // tm
    cst = lambda a: pl.BlockSpec(a.shape, lambda b, i: (0, 0))
    return pl.pallas_call(
        _attn_out_body,
        out_shape=jax.ShapeDtypeStruct((t, d), F32),
        grid=(bsz, nb),
        in_specs=[pl.BlockSpec((tm, d), lambda b, i: (b * nb + i, 0)),
                  pl.BlockSpec((None, d, tm), lambda b, i: (b, 0, i)), cst(w_o), cst(g_post)],
        out_specs=pl.BlockSpec((tm, d), lambda b, i: (b * nb + i, 0)),
        compiler_params=_params("parallel", "parallel"),
        name="attn_out",
    )(x2d, ot, w_o, g_post)


def _block_diag2(a):
    z = jnp.zeros_like(a[0])
    return jnp.concatenate([jnp.concatenate([a[0], z], axis=1), jnp.concatenate([z, a[1]], axis=1)], axis=0)


def _rope_perm():
    half = HEAD_DIM // 2
    idx = []
    for h in range(DIFF_HEADS):
        for part in range(2):
            for c in range(2):
                idx.extend(h * 2 * HEAD_DIM + c * HEAD_DIM + part * half + np.arange(half))
    return np.asarray(idx)


def _rope_tables(s_len):
    half = HEAD_DIM // 2
    inv_freq = ROPE_THETA ** (-jnp.arange(half, dtype=F32) / half)
    ang = jnp.arange(s_len, dtype=F32)[:, None] * inv_freq[None, :]
    cos, sin = jnp.cos(ang), jnp.sin(ang)
    ck = jnp.concatenate([cos] * 4, axis=1)
    sk = jnp.concatenate([-sin, -sin, sin, sin], axis=1)
    return ck, sk, ck.T, sk.T


def kernel(x_prompt, x_sample, mix0_norm_pre, mix0_norm_post, w_in0, mu_prev, mu_next, decay_w0, decay_w2, iclr_a0, iclr_a2, gate_g2, k_k, k_a, r_k, lnx_g, lnx_b, w_out0, mix1_norm_pre, mix1_norm_post, w_qkv1, lambda_q1, lambda_k1, lambda_q2, lambda_k2, subln_g, w_o1, ffn_norm_pre, ffn_norm_post, w_up, conv_w, conv_b, w_down):
    w = RWKV_WIDTH
    ones_bd = jnp.asarray(np.kron(np.eye(w // HEAD_DIM), np.ones((HEAD_DIM, HEAD_DIM))), F32).astype(BF16)
    row = lambda a: a.reshape(1, -1)

    w_in_bf = w_in0[0].astype(BF16)
    prep_consts = (row(mu_prev[0]), row(mu_next[0]),
                   row(decay_w0[0]), _block_diag2(decay_w2[0]).astype(BF16),
                   row(iclr_a0[0]), _block_diag2(iclr_a2[0]).astype(BF16),
                   gate_g2[0].astype(BF16), row(k_k[0]), row(k_a[0]), row(r_k[0]), ones_bd)
    w_out_bf = w_out0[0].astype(BF16)

    perm = _rope_perm()
    dq = DIFF_HEADS * 2 * HEAD_DIM
    wq = w_qkv1[0][:, :dq][:, perm]
    wk = w_qkv1[0][:, dq:2 * dq][:, perm]
    wv = w_qkv1[0][:, 2 * dq:]
    wk_bf, wqt_bf, wvt_bf = wk.astype(BF16), wq.T.astype(BF16), wv.T.astype(BF16)
    w_o_bf = w_o1[0].astype(BF16)
    lambda_init = 0.8 - 0.6 * math.exp(-0.3 * 1)
    sg_col = subln_g[0].reshape(-1, 1)

    w_up_bf = w_up.astype(BF16)
    w_down_bf = w_down.astype(BF16)

    def ffn(x2d, bsz, s_len, layer):
        return _ffn(x2d, bsz, s_len, row(ffn_norm_pre[layer]), w_up_bf[layer], conv_w[layer],
                    row(conv_b[layer]), w_down_bf[layer], row(ffn_norm_post[layer]))

    def trunk(x):
        bsz, s_len, d = x.shape
        x2d = x.reshape(bsz * s_len, d)
        u, z = _in_proj(x2d, mix0_norm_pre[0], w_in_bf)
        f = _fnet(u, bsz, s_len, _dft_tables(s_len))
        r, v, kn, lwf, lwb, kf, kb, bf, bb, gate, bonus = _rwkv_prep(z, bsz, s_len, prep_consts)
        yf, yb = _wkv_scan(r, v, kn, lwf, lwb, kf, kb, bf, bb, bsz, s_len)
        x2d = _mix0_out(x2d, f, yf, yb, bonus, gate, row(lnx_g[0]), row(lnx_b[0]), ones_bd, w_out_bf,
                        row(mix0_norm_post[0]))
        x2d = ffn(x2d, bsz, s_len, 0)
        k, qt, vt = _qkv(x2d, bsz, s_len, row(mix1_norm_pre[0]), wk_bf, wqt_bf, wvt_bf, _rope_tables(s_len))
        ot = _attention(qt, k, vt, row(lambda_q1[0]), row(lambda_k1[0]), row(lambda_q2[0]), row(lambda_k2[0]),
                        sg_col, lambda_init, bsz, s_len)
        x2d = _attn_out(x2d, ot, w_o_bf, row(mix1_norm_post[0]), bsz, s_len)
        x2d = ffn(x2d, bsz, s_len, 1)
        return x2d.reshape(bsz, s_len, d)

    return (trunk(x_prompt), trunk(x_sample))
```

```python
import functools
import math

import numpy as np
import jax
import jax.numpy as jnp
from jax import lax
from jax.experimental import pallas as pl
from jax.experimental.pallas import tpu as pltpu

F32 = jnp.float32
BF16 = jnp.bfloat16

D_MODEL = 1024
HEAD_DIM = 64
FNET_WIDTH = 512
RWKV_WIDTH = 512
RWKV_HEADS = 8
RWKV_IN = 1920
D_FF = 2816
NORM_EPS = 1e-6
GN_EPS = 64e-5
SUBLN_EPS = 1e-5
ROPE_THETA = 10000.0
DIFF_HEADS = 8
LANES = 128
HALO = 8
CHUNK = 64
VMEM_LIMIT = 56 * 1024 * 1024
HIGHEST = lax.Precision.HIGHEST


def _params(*sem):
    return pltpu.CompilerParams(dimension_semantics=sem, vmem_limit_bytes=VMEM_LIMIT)


def _rms(x, g, eps):
    return x * lax.rsqrt(jnp.mean(x * x, axis=-1, keepdims=True) + eps) * g


def _dot(a, b):
    return jnp.dot(a, b, preferred_element_type=F32)


def _dot_nt(a, b):
    return lax.dot_general(a, b, (((1,), (1,)), ((), ())), preferred_element_type=F32)


def _dot_tn(a, b):
    return lax.dot_general(a, b, (((0,), (0,)), ((), ())), preferred_element_type=F32)


def _sigmoid(x):
    return 1.0 / (1.0 + jnp.exp(-x))


def _segsum(x, ones_bd):
    hi = x.astype(BF16)
    lo = (x - hi.astype(F32)).astype(BF16)
    return _dot(hi, ones_bd) + _dot(lo, ones_bd)


def _in_proj_body(x_ref, g_ref, w_ref, u_ref, z_ref):
    h = _rms(x_ref[...], g_ref[...], NORM_EPS).astype(BF16)
    y = _dot(h, w_ref[...])
    u_ref[...] = y[:, :FNET_WIDTH].astype(BF16)
    z_ref[...] = y[:, FNET_WIDTH:]


def _in_proj(x2d, g, w, tm=512):
    t, d = x2d.shape
    n = w.shape[1]
    return pl.pallas_call(
        _in_proj_body,
        out_shape=(jax.ShapeDtypeStruct((t, FNET_WIDTH), BF16),
                   jax.ShapeDtypeStruct((t, n - FNET_WIDTH), F32)),
        grid=(t // tm,),
        in_specs=[pl.BlockSpec((tm, d), lambda i: (i, 0)),
                  pl.BlockSpec((1, d), lambda i: (0, 0)),
                  pl.BlockSpec((d, n), lambda i: (0, 0))],
        out_specs=(pl.BlockSpec((tm, FNET_WIDTH), lambda i: (i, 0)),
                   pl.BlockSpec((tm, n - FNET_WIDTH), lambda i: (i, 0))),
        compiler_params=_params("parallel"),
        name="in_proj",
    )(x2d, g.reshape(1, d), w)


def _rwkv_prep_body(z_ref, zp_ref, zn_ref, mup_ref, mun_ref, w0_ref, w2_ref, a0_ref, a2_ref, g2_ref,
                    kk_ref, ka_ref, rk_ref, ones_ref,
                    r_o, v_o, kn_o, lwf_o, lwb_o, kf_o, kb_o, bf_o, bb_o, gate_o, bonus_o):
    i = pl.program_id(1)
    n = pl.num_programs(1)
    z = z_ref[...]
    tm = z.shape[0]
    rows = lax.broadcasted_iota(jnp.int32, z.shape, 0)
    prev_row = jnp.where(i == 0, 0.0, zp_ref[HALO - 1:HALO, :])
    next_row = jnp.where(i == n - 1, 0.0, zn_ref[0:1, :])
    z_prev = jnp.where(rows == 0, prev_row, pltpu.roll(z, 1, 0))
    z_next = jnp.where(rows == tm - 1, next_row, pltpu.roll(z, tm - 1, 0))
    zs = z + mup_ref[...] * (z_prev - z) + mun_ref[...] * (z_next - z)

    w = RWKV_WIDTH
    r = zs[:, 0:w]
    k = zs[:, w:2 * w]
    v = zs[:, 2 * w:3 * w]
    wd = zs[:, 3 * w:3 * w + LANES]
    ad = zs[:, 3 * w + LANES:3 * w + 2 * LANES]
    gd = zs[:, 3 * w + 2 * LANES:3 * w + 3 * LANES]

    yw = w0_ref[...] + _dot(jnp.tanh(wd).astype(BF16), w2_ref[...])
    lw = -math.exp(-0.5) * _sigmoid(yw)
    iclr = _sigmoid(a0_ref[...] + _dot(ad.astype(BF16), a2_ref[...]))
    gate = _dot(_sigmoid(gd).astype(BF16), g2_ref[...])

    ones_bd = ones_ref[...]
    kk = k * kk_ref[...]
    kn = kk / jnp.maximum(jnp.sqrt(_segsum(kk * kk, ones_bd)), 1e-12)
    ka = ka_ref[...]
    k_f = k * (1.0 + (iclr[:, :w] - 1.0) * ka)
    k_b = k * (1.0 + (iclr[:, w:] - 1.0) * ka)
    bonus = _segsum(r * rk_ref[...] * (k_f + k_b), ones_bd)

    r_o[...] = r
    v_o[...] = v
    kn_o[...] = kn
    lwf_o[...] = lw[:, :w]
    lwb_o[...] = lw[:, w:]
    kf_o[...] = k_f
    kb_o[...] = k_b
    bf_o[...] = kn * iclr[:, :w]
    bb_o[...] = kn * iclr[:, w:]
    gate_o[...] = gate
    bonus_o[...] = bonus * v


def _rwkv_prep(z, bsz, s_len, consts, tm=256):
    t, n = z.shape
    nb = s_len // tm
    nh = t // HALO
    w = RWKV_WIDTH

    def full(a):
        return pl.BlockSpec(a.shape, lambda b, i: (0,) * a.ndim)

    out = jax.ShapeDtypeStruct((t, w), F32)
    return pl.pallas_call(
        _rwkv_prep_body,
        out_shape=(out,) * 11,
        grid=(bsz, nb),
        in_specs=[pl.BlockSpec((tm, n), lambda b, i: (b * nb + i, 0)),
                  pl.BlockSpec((HALO, n), lambda b, i: (jnp.maximum((b * nb + i) * (tm // HALO) - 1, 0), 0)),
                  pl.BlockSpec((HALO, n), lambda b, i: (jnp.minimum((b * nb + i + 1) * (tm // HALO), nh - 1), 0)),
                  ] + [full(a) for a in consts],
        out_specs=(pl.BlockSpec((tm, w), lambda b, i: (b * nb + i, 0)),) * 11,
        compiler_params=_params("parallel", "parallel"),
        name="rwkv_prep",
    )(z, z, z, *consts)


def _wkv_items(reverse, r_ref, v_ref, kn_ref, lw_ref, k_ref, b_ref, tri_ref, y_ref, ht_ref):
    c = CHUNK
    half = LANES // 2
    row2 = lax.broadcasted_iota(jnp.int32, (2 * c, LANES), 0)
    lane2 = lax.broadcasted_iota(jnp.int32, (2 * c, LANES), 1)
    t_row, t_col = row2 % c, lane2 % c
    own = (row2 // c) == (lane2 // HEAD_DIM)
    first = row2 < c
    order = (t_row < t_col, t_row <= t_col) if reverse else (t_row > t_col, t_row >= t_col)
    strict, incl = own & order[0], own & order[1]
    lane = lax.broadcasted_iota(jnp.int32, (c, LANES), 1)
    m0 = lane < HEAD_DIM

    def stack(x):
        return jnp.concatenate([jnp.where(m0, x, 0.0), jnp.where(m0, 0.0, x)], axis=0)

    lw_all = lw_ref[...]
    cum_all = jnp.dot(tri_ref[...], lw_all, precision=HIGHEST, preferred_element_type=F32)
    items = []
    for p in range(RWKV_WIDTH // LANES):
        sl = slice(p * LANES, (p + 1) * LANES)
        lw = lw_all[:, sl]
        cum = cum_all[:, sl]
        tot = jnp.sum(lw, axis=0, keepdims=True)
        r, v, kn, k, b = r_ref[:, sl], v_ref[:, sl], kn_ref[:, sl], k_ref[:, sl], b_ref[:, sl]
        e_neg = jnp.exp(-cum)
        e_end = jnp.exp(tot - cum)
        am2 = stack(-kn * jnp.exp(cum - lw))
        rm2 = stack(r * jnp.exp(cum))
        vm2 = stack(v).astype(BF16)
        rhs = jnp.concatenate([b * e_neg, k * e_neg], axis=0).astype(BF16)
        aa = _dot_nt(jnp.concatenate([am2, rm2], axis=0).astype(BF16), rhs)
        ar = pltpu.roll(aa, half, 1)
        aa_a, ar_a, aa_r, ar_r = aa[:2 * c], ar[:2 * c], aa[2 * c:], ar[2 * c:]
        items.append(dict(
            p=p, sl=sl, y_ref=y_ref, ht_ref=ht_ref, own=own, v=v, am2=am2, rm2=rm2, vm2=vm2, etot=jnp.exp(tot),
            bk=jnp.concatenate([b * e_end, k * e_end], axis=0).astype(BF16),
            abd_b=jnp.where(strict, jnp.where(first, aa_a, ar_a), 0.0),
            abd_k=jnp.where(strict, jnp.where(first, ar_a, aa_a), 0.0).astype(BF16),
            rbd=jnp.concatenate([jnp.where(incl, jnp.where(first, aa_r, ar_r), 0.0),
                                 jnp.where(incl, jnp.where(first, ar_r, aa_r), 0.0)], axis=1).astype(BF16)))
    return items


def _wkv_solve(items):
    half = LANES // 2
    for it in items:
        it["x"] = it["am2"] + pltpu.roll(_dot(it["abd_k"], it["vm2"]), half, 1)
        it["pw"] = it["abd_b"]
    n_lvl = int(math.log2(CHUNK))
    for lvl in range(n_lvl):
        for it in items:
            x, pw = it["x"], it["pw"].astype(BF16)
            if lvl < n_lvl - 1:
                pr = _dot(pw, jnp.concatenate([x, it["pw"]], axis=1).astype(BF16))
                it["x"] = x + pr[:, :LANES]
                it["pw"] = pr[:, LANES:]
            else:
                it["x"] = x + _dot(pw, x.astype(BF16))


def _wkv_finish(items):
    c = CHUNK
    half = LANES // 2
    r128 = lax.broadcasted_iota(jnp.int32, (LANES, LANES), 0)
    c128 = lax.broadcasted_iota(jnp.int32, (LANES, LANES), 1)
    same_head = (r128 // HEAD_DIM) == (c128 // HEAD_DIM)
    for it in items:
        x, own, rbd, bk = it["x"], it["own"], it["rbd"], it["bk"]
        p1_2 = jnp.where(own, x, 0.0)
        p2_2 = jnp.where(own, pltpu.roll(x, half, 1), 0.0)
        q1_2 = it["rm2"] + _dot(rbd[:, :LANES], p1_2.astype(BF16))
        q2_2 = _dot(rbd, jnp.concatenate([p2_2.astype(BF16), it["vm2"]], axis=0))
        p1, p2 = p1_2[:c] + p1_2[c:], p2_2[:c] + p2_2[c:]
        q1, q2 = q1_2[:c] + q1_2[c:], q2_2[:c] + q2_2[c:]
        bp = jnp.where(same_head, _dot_tn(bk[:c], p1.astype(BF16)), 0.0)
        gt = jnp.where(same_head, _dot_tn(jnp.concatenate([p2, it["v"]], axis=0).astype(BF16), bk), 0.0)
        ht = it["ht_ref"][it["p"]]
        hb = ht.astype(BF16)
        it["y_ref"][:, it["sl"]] = _dot_nt(q1.astype(BF16), hb) + q2
        it["ht_ref"][it["p"]] = it["etot"] * ht + _dot_nt(hb, bp.astype(BF16)) + gt


def _wkv_body(rf, vf, knf, lwf, kf, bf, rb, vb, knb, lwb, kb, bb, trif, trib, yf, yb, hf_ref, hb_ref):
    @pl.when(pl.program_id(1) == 0)
    def _():
        hf_ref[...] = jnp.zeros_like(hf_ref)
        hb_ref[...] = jnp.zeros_like(hb_ref)

    items = (_wkv_items(False, rf, vf, knf, lwf, kf, bf, trif, yf, hf_ref)
             + _wkv_items(True, rb, vb, knb, lwb, kb, bb, trib, yb, hb_ref))
    _wkv_solve(items)
    _wkv_finish(items)


def _wkv_scan(r, v, kn, lwf, lwb, kf, kb, bf, bb, bsz, s_len):
    t, w = r.shape
    c = CHUNK
    nc = s_len // c
    tri = np.tril(np.ones((c, c), np.float32))
    fwd = pl.BlockSpec((c, w), lambda b, i: (b * nc + i, 0))
    bwd = pl.BlockSpec((c, w), lambda b, i: (b * nc + nc - 1 - i, 0))
    cst = pl.BlockSpec((c, c), lambda b, i: (0, 0))
    out = jax.ShapeDtypeStruct((t, w), F32)
    n_pairs = w // LANES
    return pl.pallas_call(
        _wkv_body,
        out_shape=(out, out),
        grid=(bsz, nc),
        in_specs=[fwd] * 6 + [bwd] * 6 + [cst, cst],
        out_specs=(fwd, bwd),
        scratch_shapes=[pltpu.VMEM((n_pairs, LANES, LANES), F32),
                        pltpu.VMEM((n_pairs, LANES, LANES), F32)],
        compiler_params=_params("parallel", "arbitrary"),
        name="wkv_scan",
    )(r, v, kn, lwf, kf, bf, r, v, kn, lwb, kb, bb, jnp.asarray(tri), jnp.asarray(tri.T))


def _dft_split(s_len):
    s1 = 1 << ((s_len.bit_length() - 1 + 1) // 2)
    return s1, s_len // s1


def _dft_tables(s_len):
    s1, s2 = _dft_split(s_len)

    def cs(num, period):
        ang = (num % period).astype(F32) * (2.0 * math.pi / period)
        return jnp.cos(ang), jnp.sin(ang)

    k1 = jnp.arange(s1, dtype=jnp.int32)
    c1, sn1 = cs(k1[:, None] * k1[None, :], s1)
    w1 = jnp.concatenate([c1, sn1], axis=0)
    k = k1[:, None, None] + s1 * jnp.arange(s2, dtype=jnp.int32)[None, :, None]
    ec, es = cs(k * jnp.arange(s2, dtype=jnp.int32)[None, None, :], s_len)
    e = jnp.concatenate([jnp.concatenate([ec, -es], axis=2),
                         jnp.concatenate([-es, -ec], axis=2)], axis=1)
    c = jnp.arange(HEAD_DIM, dtype=jnp.int32)
    c3, s3 = cs(c[:, None] * c[None, :], HEAD_DIM)
    eye = jnp.eye(2 * LANES // HEAD_DIM, dtype=F32)
    return (w1.astype(BF16), e.astype(BF16), jnp.kron(eye, c3).astype(BF16), jnp.kron(eye, s3).astype(BF16))


def _dft1_body(w_ref, u_ref, y_ref):
    y_ref[...] = _dot(w_ref[...], u_ref[...]).astype(BF16)


def _dft2_body(scale, e_ref, y_ref, c4_ref, s4_ref, f_ref):
    g = e_ref.shape[0]
    s2 = y_ref.shape[2]
    nw = 2 * LANES
    for j in range(g):
        yy = jnp.concatenate([y_ref[0, j], y_ref[1, j]], axis=0)
        x = _dot(e_ref[j], yy)
        xr = x[:s2].astype(BF16)
        xi = x[s2:].astype(BF16)
        for q in range(FNET_WIDTH // nw):
            f = _dot(xr[:, q * nw:(q + 1) * nw], c4_ref[...]) + _dot(xi[:, q * nw:(q + 1) * nw], s4_ref[...])
            f_ref[:, j * FNET_WIDTH + q * nw:j * FNET_WIDTH + (q + 1) * nw] = f * scale


def _fnet(u, bsz, s_len, tables):
    w1, e, c4, s4 = tables
    s1, s2 = _dft_split(s_len)
    fw = FNET_WIDTH
    ncol = s2 * fw
    nb = min(ncol, 4096)
    u3 = u.reshape(bsz, s1, ncol)
    y = pl.pallas_call(
        _dft1_body,
        out_shape=jax.ShapeDtypeStruct((bsz, 2 * s1, ncol), BF16),
        grid=(bsz, ncol // nb),
        in_specs=[pl.BlockSpec((2 * s1, s1), lambda b, j: (0, 0)),
                  pl.BlockSpec((None, s1, nb), lambda b, j: (b, 0, j))],
        out_specs=pl.BlockSpec((None, 2 * s1, nb), lambda b, j: (b, 0, j)),
        compiler_params=_params("parallel", "parallel"),
        name="dft_stage1",
    )(w1, u3)
    y5 = y.reshape(bsz, 2, s1, s2, fw)
    g = 8
    scale = 1.0 / math.sqrt(s_len * HEAD_DIM)
    f = pl.pallas_call(
        functools.partial(_dft2_body, scale),
        out_shape=jax.ShapeDtypeStruct((bsz, s2, s1 * fw), F32),
        grid=(bsz, s1 // g),
        in_specs=[pl.BlockSpec((g, 2 * s2, 2 * s2), lambda b, j: (j, 0, 0)),
                  pl.BlockSpec((None, 2, g, s2, fw), lambda b, j: (b, 0, j, 0, 0)),
                  pl.BlockSpec(c4.shape, lambda b, j: (0, 0)),
                  pl.BlockSpec(s4.shape, lambda b, j: (0, 0))],
        out_specs=pl.BlockSpec((None, s2, g * fw), lambda b, j: (b, 0, j)),
        compiler_params=_params("parallel", "parallel"),
        name="dft_stage2",
    )(e, y5, c4, s4)
    return f.reshape(bsz * s_len, fw)


def _mix0_out_body(x_ref, f_ref, yf_ref, yb_ref, bonus_ref, gate_ref, lng_ref, lnb_ref, ones_ref, w_ref, g_ref,
                   o_ref):
    ones_bd = ones_ref[...]
    y = yf_ref[...] + yb_ref[...]
    mu = _segsum(y, ones_bd) * (1.0 / HEAD_DIM)
    d = y - mu
    var = _segsum(d * d, ones_bd) * (1.0 / HEAD_DIM)
    yn = d * lax.rsqrt(var + GN_EPS) * lng_ref[...] + lnb_ref[...]
    o = (yn + bonus_ref[...]) * gate_ref[...]
    cat = jnp.concatenate([f_ref[...].astype(BF16), o.astype(BF16)], axis=1)
    h = _dot(cat, w_ref[...])
    o_ref[...] = x_ref[...] + _rms(h, g_ref[...], NORM_EPS)


def _mix0_out(x2d, f, yf, yb, bonus, gate, lng, lnb, ones_bd, w_out, g_post, tm=512):
    t, d = x2d.shape
    w = RWKV_WIDTH
    row = lambda n: pl.BlockSpec((tm, n), lambda i: (i, 0))
    cst = lambda a: pl.BlockSpec(a.shape, lambda i: (0, 0))
    return pl.pallas_call(
        _mix0_out_body,
        out_shape=jax.ShapeDtypeStruct((t, d), F32),
        grid=(t // tm,),
        in_specs=[row(d), row(w), row(w), row(w), row(w), row(w),
                  cst(lng), cst(lnb), cst(ones_bd), cst(w_out), cst(g_post)],
        out_specs=row(d),
        compiler_params=_params("parallel"),
        name="mix0_out",
    )(x2d, f, yf, yb, bonus, gate, lng, lnb, ones_bd, w_out, g_post)


FF_CHUNK = 256


def _ffn_body(x_ref, xp_ref, xn_ref, gpre_ref, wup_ref, cw_ref, cb_ref, wdn_ref, gpost_ref, o_ref,
              h_ref, acc_ref):
    i = pl.program_id(1)
    n = pl.num_programs(1)
    tm = x_ref.shape[0]
    g = gpre_ref[...]
    x = x_ref[...]
    hp = jnp.where(i == 0, 0.0, _rms(xp_ref[...], g, NORM_EPS))
    hn = jnp.where(i == n - 1, 0.0, _rms(xn_ref[...], g, NORM_EPS))
    h_ref[...] = jnp.concatenate([hp, _rms(x, g, NORM_EPS), hn], axis=0).astype(BF16)
    acc_ref[...] = jnp.zeros_like(acc_ref)

    def chunk(j, carry):
        cv = pl.multiple_of(j * FF_CHUNK, FF_CHUNK)
        cg = pl.multiple_of(D_FF + j * FF_CHUNK, FF_CHUNK)
        h = h_ref[...]

        def conv(c0):
            u = _dot(h, wup_ref[:, pl.ds(c0, FF_CHUNK)])
            cw = cw_ref[:, pl.ds(c0, FF_CHUNK)]
            return (cw[0:1] * u[HALO - 1:HALO - 1 + tm] + cw[1:2] * u[HALO:HALO + tm]
                    + cw[2:3] * u[HALO + 1:HALO + 1 + tm] + cb_ref[:, pl.ds(c0, FF_CHUNK)])

        val = conv(cv)
        gate = conv(cg)
        act = 0.5 * gate * (1.0 + jnp.tanh(math.sqrt(2.0 / math.pi) * (gate + 0.044715 * gate * gate * gate)))
        acc_ref[...] += _dot((act * val).astype(BF16), wdn_ref[pl.ds(cv, FF_CHUNK), :])
        return carry

    lax.fori_loop(0, D_FF // FF_CHUNK, chunk, 0, unroll=True)
    o_ref[...] = x + _rms(acc_ref[...], gpost_ref[...], NORM_EPS)


def _ffn(x2d, bsz, s_len, g_pre, w_up, conv_w, conv_b, w_down, g_post, tm=512):
    t, d = x2d.shape
    nb = s_len // tm
    nh = t // HALO
    cst = lambda a: pl.BlockSpec(a.shape, lambda b, i: (0,) * a.ndim, pipeline_mode=pl.Buffered(1))
    return pl.pallas_call(
        _ffn_body,
        out_shape=jax.ShapeDtypeStruct((t, d), F32),
        grid=(bsz, nb),
        in_specs=[pl.BlockSpec((tm, d), lambda b, i: (b * nb + i, 0)),
                  pl.BlockSpec((HALO, d), lambda b, i: (jnp.maximum((b * nb + i) * (tm // HALO) - 1, 0), 0)),
                  pl.BlockSpec((HALO, d), lambda b, i: (jnp.minimum((b * nb + i + 1) * (tm // HALO), nh - 1), 0)),
                  cst(g_pre), cst(w_up), cst(conv_w), cst(conv_b), cst(w_down), cst(g_post)],
        out_specs=pl.BlockSpec((tm, d), lambda b, i: (b * nb + i, 0)),
        scratch_shapes=[pltpu.VMEM((tm + 2 * HALO, d), BF16), pltpu.VMEM((tm, d), F32)],
        compiler_params=_params("parallel", "parallel"),
        name="conv_ffn",
    )(x2d, x2d, x2d, g_pre, w_up, conv_w, conv_b, w_down, g_post)


def _qkv_body(x_ref, g_ref, wk_ref, wqt_ref, wvt_ref, ck_ref, sk_ref, cq_ref, sq_ref, k_o, qt_o, vt_o):
    h = _rms(x_ref[...], g_ref[...], NORM_EPS).astype(BF16)
    ck, sk = ck_ref[...], sk_ref[...]
    cq, sq = cq_ref[...], sq_ref[...]
    half = LANES // 2
    for hd in range(DIFF_HEADS):
        sl = slice(hd * LANES, (hd + 1) * LANES)
        kh = _dot(h, wk_ref[:, sl])
        k_o[:, sl] = (kh * ck + pltpu.roll(kh, half, 1) * sk).astype(BF16)
        qh = _dot_nt(wqt_ref[sl, :], h)
        qsw = jnp.concatenate([qh[half:], qh[:half]], axis=0)
        qt_o[sl, :] = ((qh * cq + qsw * sq) * (HEAD_DIM ** -0.5 * math.log2(math.e))).astype(BF16)
        vt_o[sl, :] = _dot_nt(wvt_ref[sl, :], h).astype(BF16)


def _qkv(x2d, bsz, s_len, g, wk, wqt, wvt, tabs, tm=512):
    t, d = x2d.shape
    nb = s_len // tm
    ck, sk, cq, sq = tabs
    cst = lambda a: pl.BlockSpec(a.shape, lambda b, i: (0, 0))
    tshape = jax.ShapeDtypeStruct((bsz, d, s_len), BF16)
    return pl.pallas_call(
        _qkv_body,
        out_shape=(jax.ShapeDtypeStruct((t, d), BF16), tshape, tshape),
        grid=(bsz, nb),
        in_specs=[pl.BlockSpec((tm, d), lambda b, i: (b * nb + i, 0)), cst(g), cst(wk), cst(wqt), cst(wvt),
                  pl.BlockSpec((tm, LANES), lambda b, i: (i, 0)),
                  pl.BlockSpec((tm, LANES), lambda b, i: (i, 0)),
                  pl.BlockSpec((LANES, tm), lambda b, i: (0, i)),
                  pl.BlockSpec((LANES, tm), lambda b, i: (0, i))],
        out_specs=(pl.BlockSpec((tm, d), lambda b, i: (b * nb + i, 0)),
                   pl.BlockSpec((None, d, tm), lambda b, i: (b, 0, i)),
                   pl.BlockSpec((None, d, tm), lambda b, i: (b, 0, i))),
        compiler_params=_params("parallel", "parallel"),
        name="qkv_rope",
    )(x2d, g, wk, wqt, wvt, ck, sk, cq, sq)


NEG_BIG = -1e30


def _attn_body(lambda_init, tk, qt_ref, k_ref, vt_ref, lq1, lk1, lq2, lk2, sg_ref, o_ref, sa_ref, sb_ref, acc_ref):
    qt = qt_ref[...]
    tq = qt.shape[1]
    s_len = k_ref.shape[0]
    rowi = lax.broadcasted_iota(jnp.int32, qt.shape, 0)
    first = (rowi // (HEAD_DIM // 2)) % 2 == 0
    zero = jnp.zeros_like(qt)
    qh = (jnp.where(first, qt, zero), jnp.where(first, zero, qt))

    def scores(j, dst):
        kk = k_ref[pl.ds(pl.multiple_of(j * tk, tk), tk), :]
        for c in range(2):
            dst[c] = _dot(kk, qh[c])

    def update(j, src, ml):
        vt = vt_ref[:, pl.ds(pl.multiple_of(j * tk, tk), tk)]
        out = []
        for c in range(2):
            m, l = ml[c]
            s = src[c]
            mn = jnp.maximum(m, jnp.max(s, axis=0, keepdims=True))
            al = jnp.exp2(m - mn)
            p = jnp.exp2(s - mn)
            l = al * l + jnp.sum(p, axis=0, keepdims=True)
            acc_ref[c] = al * acc_ref[c] + _dot(vt, p.astype(BF16))
            out.append((mn, l))
        return tuple(out)

    def step(i, ml):
        scores(2 * i + 1, sb_ref)
        ml = update(2 * i, sa_ref, ml)
        scores(2 * i + 2, sa_ref)
        return update(2 * i + 1, sb_ref, ml)

    mi = jnp.full((1, tq), NEG_BIG, F32)
    li = jnp.zeros((1, tq), F32)
    acc_ref[...] = jnp.zeros_like(acc_ref)
    n_tiles = s_len // tk
    scores(0, sa_ref)
    ml = lax.fori_loop(0, n_tiles // 2 - 1, step, ((mi, li), (mi, li)))
    scores(n_tiles - 1, sb_ref)
    ml = update(n_tiles - 2, sa_ref, ml)
    (m0, l0), (m1, l1) = update(n_tiles - 1, sb_ref, ml)
    a0, a1 = acc_ref[0], acc_ref[1]
    lam = (jnp.exp(jnp.sum(lq1[...] * lk1[...], axis=1, keepdims=True))
           - jnp.exp(jnp.sum(lq2[...] * lk2[...], axis=1, keepdims=True)) + lambda_init)
    o = a0 / l0 - lam * (a1 / l1)
    o = o * lax.rsqrt(jnp.mean(o * o, axis=0, keepdims=True) + SUBLN_EPS) * sg_ref[...] * (1.0 - lambda_init)
    o_ref[...] = o.astype(BF16)


def _attention(qt, k, vt, lq1, lk1, lq2, lk2, sg, lambda_init, bsz, s_len, tq=256, tk=512):
    d = qt.shape[1]
    assert s_len % (2 * tk) == 0 and s_len % tq == 0, (s_len, tq, tk)
    k3 = k.reshape(bsz, s_len, d)
    cst = lambda a: pl.BlockSpec(a.shape, lambda b, h, i: (0, 0))
    return pl.pallas_call(
        functools.partial(_attn_body, lambda_init, tk),
        out_shape=jax.ShapeDtypeStruct((bsz, d, s_len), BF16),
        grid=(bsz, DIFF_HEADS, s_len // tq),
        in_specs=[pl.BlockSpec((None, LANES, tq), lambda b, h, i: (b, h, i)),
                  pl.BlockSpec((None, s_len, LANES), lambda b, h, i: (b, 0, h)),
                  pl.BlockSpec((None, LANES, s_len), lambda b, h, i: (b, h, 0)),
                  cst(lq1), cst(lk1), cst(lq2), cst(lk2), cst(sg)],
        out_specs=pl.BlockSpec((None, LANES, tq), lambda b, h, i: (b, h, i)),
        scratch_shapes=[pltpu.VMEM((2, tk, tq), F32), pltpu.VMEM((2, tk, tq), F32),
                        pltpu.VMEM((2, LANES, tq), F32)],
        compiler_params=_params("parallel", "parallel", "parallel"),
        name="diff_attn",
    )(qt, k3, vt, lq1, lk1, lq2, lk2, sg)


def _attn_out_body(x_ref, ot_ref, w_ref, g_ref, o_ref):
    h = _dot_tn(ot_ref[...], w_ref[...])
    o_ref[...] = x_ref[...] + _rms(h, g_ref[...], NORM_EPS)


def _attn_out(x2d, ot, w_o, g_post, bsz, s_len, tm=512):
    t, d = x2d.shape
    nb = s_len // tm
    cst = lambda a: pl.BlockSpec(a.shape, lambda b, i: (0, 0))
    return pl.pallas_call(
        _attn_out_body,
        out_shape=jax.ShapeDtypeStruct((t, d), F32),
        grid=(bsz, nb),
        in_specs=[pl.BlockSpec((tm, d), lambda b, i: (b * nb + i, 0)),
                  pl.BlockSpec((None, d, tm), lambda b, i: (b, 0, i)), cst(w_o), cst(g_post)],
        out_specs=pl.BlockSpec((tm, d), lambda b, i: (b * nb + i, 0)),
        compiler_params=_params("parallel", "parallel"),
        name="attn_out",
    )(x2d, ot, w_o, g_post)


def _block_diag2(a):
    z = jnp.zeros_like(a[0])
    return jnp.concatenate([jnp.concatenate([a[0], z], axis=1), jnp.concatenate([z, a[1]], axis=1)], axis=0)


def _rope_perm():
    half = HEAD_DIM // 2
    idx = []
    for h in range(DIFF_HEADS):
        for part in range(2):
            for c in range(2):
                idx.extend(h * 2 * HEAD_DIM + c * HEAD_DIM + part * half + np.arange(half))
    return np.asarray(idx)


def _rope_tables(s_len):
    half = HEAD_DIM // 2
    inv_freq = ROPE_THETA ** (-jnp.arange(half, dtype=F32) / half)
    ang = jnp.arange(s_len, dtype=F32)[:, None] * inv_freq[None, :]
    cos, sin = jnp.cos(ang), jnp.sin(ang)
    ck = jnp.concatenate([cos] * 4, axis=1)
    sk = jnp.concatenate([-sin, -sin, sin, sin], axis=1)
    return ck, sk, ck.T, sk.T


def kernel(x_prompt, x_sample, mix0_norm_pre, mix0_norm_post, w_in0, mu_prev, mu_next, decay_w0, decay_w2, iclr_a0, iclr_a2, gate_g2, k_k, k_a, r_k, lnx_g, lnx_b, w_out0, mix1_norm_pre, mix1_norm_post, w_qkv1, lambda_q1, lambda_k1, lambda_q2, lambda_k2, subln_g, w_o1, ffn_norm_pre, ffn_norm_post, w_up, conv_w, conv_b, w_down):
    w = RWKV_WIDTH
    ones_bd = jnp.asarray(np.kron(np.eye(w // HEAD_DIM), np.ones((HEAD_DIM, HEAD_DIM))), F32).astype(BF16)
    row = lambda a: a.reshape(1, -1)

    w_in_bf = w_in0[0].astype(BF16)
    prep_consts = (row(mu_prev[0]), row(mu_next[0]),
                   row(decay_w0[0]), _block_diag2(decay_w2[0]).astype(BF16),
                   row(iclr_a0[0]), _block_diag2(iclr_a2[0]).astype(BF16),
                   gate_g2[0].astype(BF16), row(k_k[0]), row(k_a[0]), row(r_k[0]), ones_bd)
    w_out_bf = w_out0[0].astype(BF16)

    perm = _rope_perm()
    dq = DIFF_HEADS * 2 * HEAD_DIM
    wq = w_qkv1[0][:, :dq][:, perm]
    wk = w_qkv1[0][:, dq:2 * dq][:, perm]
    wv = w_qkv1[0][:, 2 * dq:]
    wk_bf, wqt_bf, wvt_bf = wk.astype(BF16), wq.T.astype(BF16), wv.T.astype(BF16)
    w_o_bf = w_o1[0].astype(BF16)
    lambda_init = 0.8 - 0.6 * math.exp(-0.3 * 1)
    sg_col = subln_g[0].reshape(-1, 1)

    w_up_bf = w_up.astype(BF16)
    w_down_bf = w_down.astype(BF16)

    def ffn(x2d, bsz, s_len, layer):
        return _ffn(x2d, bsz, s_len, row(ffn_norm_pre[layer]), w_up_bf[layer], conv_w[layer],
                    row(conv_b[layer]), w_down_bf[layer], row(ffn_norm_post[layer]))

    def trunk(x):
        bsz, s_len, d = x.shape
        x2d = x.reshape(bsz * s_len, d)
        u, z = _in_proj(x2d, mix0_norm_pre[0], w_in_bf)
        f = _fnet(u, bsz, s_len, _dft_tables(s_len))
        r, v, kn, lwf, lwb, kf, kb, bf, bb, gate, bonus = _rwkv_prep(z, bsz, s_len, prep_consts)
        yf, yb = _wkv_scan(r, v, kn, lwf, lwb, kf, kb, bf, bb, bsz, s_len)
        x2d = _mix0_out(x2d, f, yf, yb, bonus, gate, row(lnx_g[0]), row(lnx_b[0]), ones_bd, w_out_bf,
                        row(mix0_norm_post[0]))
        x2d = ffn(x2d, bsz, s_len, 0)
        k, qt, vt = _qkv(x2d, bsz, s_len, row(mix1_norm_pre[0]), wk_bf, wqt_bf, wvt_bf, _rope_tables(s_len))
        ot = _attention(qt, k, vt, row(lambda_q1[0]), row(lambda_k1[0]), row(lambda_q2[0]), row(lambda_k2[0]),
                        sg_col, lambda_init, bsz, s_len)
        x2d = _attn_out(x2d, ot, w_o_bf, row(mix1_norm_post[0]), bsz, s_len)
        x2d = ffn(x2d, bsz, s_len, 1)
        return x2d.reshape(bsz, s_len, d)

    return (trunk(x_prompt), trunk(x_sample))
```

```python
import functools
import math

import numpy as np
import jax
import jax.numpy as jnp
from jax import lax
from jax.experimental import pallas as pl
from jax.experimental.pallas import tpu as pltpu

F32 = jnp.float32
BF16 = jnp.bfloat16

D_MODEL = 1024
HEAD_DIM = 64
FNET_WIDTH = 512
RWKV_WIDTH = 512
RWKV_HEADS = 8
RWKV_IN = 1920
D_FF = 2816
NORM_EPS = 1e-6
GN_EPS = 64e-5
SUBLN_EPS = 1e-5
ROPE_THETA = 10000.0
DIFF_HEADS = 8
LANES = 128
HALO = 8
CHUNK = 64
WKV_CHUNKS_PER_STEP = 2
VMEM_LIMIT = 56 * 1024 * 1024
HIGHEST = lax.Precision.HIGHEST


def _params(*sem):
    return pltpu.CompilerParams(dimension_semantics=sem, vmem_limit_bytes=VMEM_LIMIT)


def _rms(x, g, eps):
    return x * lax.rsqrt(jnp.mean(x * x, axis=-1, keepdims=True) + eps) * g


def _dot(a, b):
    return jnp.dot(a, b, preferred_element_type=F32)


def _dot_nt(a, b):
    return lax.dot_general(a, b, (((1,), (1,)), ((), ())), preferred_element_type=F32)


def _dot_tn(a, b):
    return lax.dot_general(a, b, (((0,), (0,)), ((), ())), preferred_element_type=F32)


def _sigmoid(x):
    return 1.0 / (1.0 + jnp.exp(-x))


def _segsum(x, ones_bd):
    hi = x.astype(BF16)
    lo = (x - hi.astype(F32)).astype(BF16)
    return _dot(hi, ones_bd) + _dot(lo, ones_bd)


def _in_proj_body(x_ref, g_ref, w_ref, u_ref, z_ref):
    h = _rms(x_ref[...], g_ref[...], NORM_EPS).astype(BF16)
    y = _dot(h, w_ref[...])
    u_ref[...] = y[:, :FNET_WIDTH].astype(BF16)
    z_ref[...] = y[:, FNET_WIDTH:]


def _in_proj(x2d, g, w, tm=512):
    t, d = x2d.shape
    n = w.shape[1]
    return pl.pallas_call(
        _in_proj_body,
        out_shape=(jax.ShapeDtypeStruct((t, FNET_WIDTH), BF16),
                   jax.ShapeDtypeStruct((t, n - FNET_WIDTH), F32)),
        grid=(t // tm,),
        in_specs=[pl.BlockSpec((tm, d), lambda i: (i, 0)),
                  pl.BlockSpec((1, d), lambda i: (0, 0)),
                  pl.BlockSpec((d, n), lambda i: (0, 0))],
        out_specs=(pl.BlockSpec((tm, FNET_WIDTH), lambda i: (i, 0)),
                   pl.BlockSpec((tm, n - FNET_WIDTH), lambda i: (i, 0))),
        compiler_params=_params("parallel"),
        name="in_proj",
    )(x2d, g.reshape(1, d), w)


def _rwkv_prep_body(z_ref, zp_ref, zn_ref, mup_ref, mun_ref, w0_ref, w2_ref, a0_ref, a2_ref, g2_ref,
                    kk_ref, ka_ref, rk_ref, ones_ref,
                    r_o, v_o, kn_o, lwf_o, lwb_o, kf_o, kb_o, bf_o, bb_o, gate_o, bonus_o):
    i = pl.program_id(1)
    n = pl.num_programs(1)
    z = z_ref[...]
    tm = z.shape[0]
    rows = lax.broadcasted_iota(jnp.int32, z.shape, 0)
    prev_row = jnp.where(i == 0, 0.0, zp_ref[HALO - 1:HALO, :])
    next_row = jnp.where(i == n - 1, 0.0, zn_ref[0:1, :])
    z_prev = jnp.where(rows == 0, prev_row, pltpu.roll(z, 1, 0))
    z_next = jnp.where(rows == tm - 1, next_row, pltpu.roll(z, tm - 1, 0))
    zs = z + mup_ref[...] * (z_prev - z) + mun_ref[...] * (z_next - z)

    w = RWKV_WIDTH
    r = zs[:, 0:w]
    k = zs[:, w:2 * w]
    v = zs[:, 2 * w:3 * w]
    wd = zs[:, 3 * w:3 * w + LANES]
    ad = zs[:, 3 * w + LANES:3 * w + 2 * LANES]
    gd = zs[:, 3 * w + 2 * LANES:3 * w + 3 * LANES]

    yw = w0_ref[...] + _dot(jnp.tanh(wd).astype(BF16), w2_ref[...])
    lw = -math.exp(-0.5) * _sigmoid(yw)
    iclr = _sigmoid(a0_ref[...] + _dot(ad.astype(BF16), a2_ref[...]))
    gate = _dot(_sigmoid(gd).astype(BF16), g2_ref[...])

    ones_bd = ones_ref[...]
    kk = k * kk_ref[...]
    kn = kk / jnp.maximum(jnp.sqrt(_segsum(kk * kk, ones_bd)), 1e-12)
    ka = ka_ref[...]
    k_f = k * (1.0 + (iclr[:, :w] - 1.0) * ka)
    k_b = k * (1.0 + (iclr[:, w:] - 1.0) * ka)
    bonus = _segsum(r * rk_ref[...] * (k_f + k_b), ones_bd)

    r_o[...] = r
    v_o[...] = v
    kn_o[...] = kn
    lwf_o[...] = lw[:, :w]
    lwb_o[...] = lw[:, w:]
    kf_o[...] = k_f
    kb_o[...] = k_b
    bf_o[...] = kn * iclr[:, :w]
    bb_o[...] = kn * iclr[:, w:]
    gate_o[...] = gate
    bonus_o[...] = bonus * v


def _rwkv_prep(z, bsz, s_len, consts, tm=256):
    t, n = z.shape
    nb = s_len // tm
    nh = t // HALO
    w = RWKV_WIDTH

    def full(a):
        return pl.BlockSpec(a.shape, lambda b, i: (0,) * a.ndim)

    out = jax.ShapeDtypeStruct((t, w), F32)
    return pl.pallas_call(
        _rwkv_prep_body,
        out_shape=(out,) * 11,
        grid=(bsz, nb),
        in_specs=[pl.BlockSpec((tm, n), lambda b, i: (b * nb + i, 0)),
                  pl.BlockSpec((HALO, n), lambda b, i: (jnp.maximum((b * nb + i) * (tm // HALO) - 1, 0), 0)),
                  pl.BlockSpec((HALO, n), lambda b, i: (jnp.minimum((b * nb + i + 1) * (tm // HALO), nh - 1), 0)),
                  ] + [full(a) for a in consts],
        out_specs=(pl.BlockSpec((tm, w), lambda b, i: (b * nb + i, 0)),) * 11,
        compiler_params=_params("parallel", "parallel"),
        name="rwkv_prep",
    )(z, z, z, *consts)


def _wkv_items(reverse, r_ref, v_ref, kn_ref, lw_ref, k_ref, b_ref, tri_ref, y_ref, ht_ref):
    c = CHUNK
    half = LANES // 2
    row2 = lax.broadcasted_iota(jnp.int32, (2 * c, LANES), 0)
    lane2 = lax.broadcasted_iota(jnp.int32, (2 * c, LANES), 1)
    t_row, t_col = row2 % c, lane2 % c
    own = (row2 // c) == (lane2 // HEAD_DIM)
    first = row2 < c
    order = (t_row < t_col, t_row <= t_col) if reverse else (t_row > t_col, t_row >= t_col)
    strict, incl = own & order[0], own & order[1]
    lane = lax.broadcasted_iota(jnp.int32, (c, LANES), 1)
    m0 = lane < HEAD_DIM

    def stack(x):
        return jnp.concatenate([jnp.where(m0, x, 0.0), jnp.where(m0, 0.0, x)], axis=0)

    cum_blk = jnp.dot(tri_ref[...], lw_ref[...], precision=HIGHEST, preferred_element_type=F32)
    items = []
    subs = range(WKV_CHUNKS_PER_STEP)
    for sub, p in [(s, p) for s in (reversed(subs) if reverse else subs) for p in range(RWKV_WIDTH // LANES)]:
        rows = slice(sub * c, (sub + 1) * c)
        sl = slice(p * LANES, (p + 1) * LANES)
        lw = lw_ref[rows, sl]
        cum = cum_blk[rows, sl]
        tot = jnp.sum(lw, axis=0, keepdims=True)
        r, v, kn, k, b = r_ref[rows, sl], v_ref[rows, sl], kn_ref[rows, sl], k_ref[rows, sl], b_ref[rows, sl]
        e_neg = jnp.exp(-cum)
        e_end = jnp.exp(tot - cum)
        am2 = stack(-kn * jnp.exp(cum - lw))
        rm2 = stack(r * jnp.exp(cum))
        vm2 = stack(v).astype(BF16)
        rhs = jnp.concatenate([b * e_neg, k * e_neg], axis=0).astype(BF16)
        aa = _dot_nt(jnp.concatenate([am2, rm2], axis=0).astype(BF16), rhs)
        ar = pltpu.roll(aa, half, 1)
        aa_a, ar_a, aa_r, ar_r = aa[:2 * c], ar[:2 * c], aa[2 * c:], ar[2 * c:]
        items.append(dict(
            p=p, rows=rows, sl=sl, y_ref=y_ref, ht_ref=ht_ref, own=own, v=v, am2=am2, rm2=rm2, vm2=vm2,
            etot=jnp.exp(tot),
            bk=jnp.concatenate([b * e_end, k * e_end], axis=0).astype(BF16),
            abd_b=jnp.where(strict, jnp.where(first, aa_a, ar_a), 0.0),
            abd_k=jnp.where(strict, jnp.where(first, ar_a, aa_a), 0.0).astype(BF16),
            rbd=jnp.concatenate([jnp.where(incl, jnp.where(first, aa_r, ar_r), 0.0),
                                 jnp.where(incl, jnp.where(first, ar_r, aa_r), 0.0)], axis=1).astype(BF16)))
    return items


def _wkv_solve(items):
    half = LANES // 2
    for it in items:
        it["x"] = it["am2"] + pltpu.roll(_dot(it["abd_k"], it["vm2"]), half, 1)
        it["pw"] = it["abd_b"]
    n_lvl = int(math.log2(CHUNK))
    for lvl in range(n_lvl):
        for it in items:
            x, pw = it["x"], it["pw"].astype(BF16)
            if lvl < n_lvl - 1:
                pr = _dot(pw, jnp.concatenate([x, it["pw"]], axis=1).astype(BF16))
                it["x"] = x + pr[:, :LANES]
                it["pw"] = pr[:, LANES:]
            else:
                it["x"] = x + _dot(pw, x.astype(BF16))


def _wkv_finish(items):
    c = CHUNK
    half = LANES // 2
    r128 = lax.broadcasted_iota(jnp.int32, (LANES, LANES), 0)
    c128 = lax.broadcasted_iota(jnp.int32, (LANES, LANES), 1)
    same_head = (r128 // HEAD_DIM) == (c128 // HEAD_DIM)
    for it in items:
        x, own, rbd, bk = it["x"], it["own"], it["rbd"], it["bk"]
        p1_2 = jnp.where(own, x, 0.0)
        p2_2 = jnp.where(own, pltpu.roll(x, half, 1), 0.0)
        q1_2 = it["rm2"] + _dot(rbd[:, :LANES], p1_2.astype(BF16))
        q2_2 = _dot(rbd, jnp.concatenate([p2_2.astype(BF16), it["vm2"]], axis=0))
        p1, p2 = p1_2[:c] + p1_2[c:], p2_2[:c] + p2_2[c:]
        q1, q2 = q1_2[:c] + q1_2[c:], q2_2[:c] + q2_2[c:]
        bp = jnp.where(same_head, _dot_tn(bk[:c], p1.astype(BF16)), 0.0)
        gt = jnp.where(same_head, _dot_tn(jnp.concatenate([p2, it["v"]], axis=0).astype(BF16), bk), 0.0)
        ht = it["ht_ref"][it["p"]]
        hb = ht.astype(BF16)
        it["y_ref"][it["rows"], it["sl"]] = _dot_nt(q1.astype(BF16), hb) + q2
        it["ht_ref"][it["p"]] = it["etot"] * ht + _dot_nt(hb, bp.astype(BF16)) + gt


def _wkv_body(rf, vf, knf, lwf, kf, bf, rb, vb, knb, lwb, kb, bb, trif, trib, yf, yb, hf_ref, hb_ref):
    @pl.when(pl.program_id(1) == 0)
    def _():
        hf_ref[...] = jnp.zeros_like(hf_ref)
        hb_ref[...] = jnp.zeros_like(hb_ref)

    items = (_wkv_items(False, rf, vf, knf, lwf, kf, bf, trif, yf, hf_ref)
             + _wkv_items(True, rb, vb, knb, lwb, kb, bb, trib, yb, hb_ref))
    _wkv_solve(items)
    _wkv_finish(items)


def _wkv_scan(r, v, kn, lwf, lwb, kf, kb, bf, bb, bsz, s_len):
    t, w = r.shape
    c = CHUNK * WKV_CHUNKS_PER_STEP
    nc = s_len // c
    tri = np.kron(np.eye(WKV_CHUNKS_PER_STEP, dtype=np.float32), np.tril(np.ones((CHUNK, CHUNK), np.float32)))
    fwd = pl.BlockSpec((c, w), lambda b, i: (b * nc + i, 0))
    bwd = pl.BlockSpec((c, w), lambda b, i: (b * nc + nc - 1 - i, 0))
    cst = pl.BlockSpec((c, c), lambda b, i: (0, 0))
    out = jax.ShapeDtypeStruct((t, w), F32)
    n_pairs = w // LANES
    return pl.pallas_call(
        _wkv_body,
        out_shape=(out, out),
        grid=(bsz, nc),
        in_specs=[fwd] * 6 + [bwd] * 6 + [cst, cst],
        out_specs=(fwd, bwd),
        scratch_shapes=[pltpu.VMEM((n_pairs, LANES, LANES), F32),
                        pltpu.VMEM((n_pairs, LANES, LANES), F32)],
        compiler_params=_params("parallel", "arbitrary"),
        name="wkv_scan",
    )(r, v, kn, lwf, kf, bf, r, v, kn, lwb, kb, bb, jnp.asarray(tri), jnp.asarray(tri.T))


def _dft_split(s_len):
    s1 = 1 << ((s_len.bit_length() - 1 + 1) // 2)
    return s1, s_len // s1


def _dft_tables(s_len):
    s1, s2 = _dft_split(s_len)

    def cs(num, period):
        ang = (num % period).astype(F32) * (2.0 * math.pi / period)
        return jnp.cos(ang), jnp.sin(ang)

    k1 = jnp.arange(s1, dtype=jnp.int32)
    c1, sn1 = cs(k1[:, None] * k1[None, :], s1)
    w1 = jnp.concatenate([c1, sn1], axis=0)
    k = k1[:, None, None] + s1 * jnp.arange(s2, dtype=jnp.int32)[None, :, None]
    ec, es = cs(k * jnp.arange(s2, dtype=jnp.int32)[None, None, :], s_len)
    e = jnp.concatenate([jnp.concatenate([ec, -es], axis=2),
                         jnp.concatenate([-es, -ec], axis=2)], axis=1)
    c = jnp.arange(HEAD_DIM, dtype=jnp.int32)
    c3, s3 = cs(c[:, None] * c[None, :], HEAD_DIM)
    eye = jnp.eye(2 * LANES // HEAD_DIM, dtype=F32)
    return (w1.astype(BF16), e.astype(BF16), jnp.kron(eye, c3).astype(BF16), jnp.kron(eye, s3).astype(BF16))


def _dft1_body(w_ref, u_ref, y_ref):
    y_ref[...] = _dot(w_ref[...], u_ref[...]).astype(BF16)


def _dft2_body(scale, e_ref, y_ref, c4_ref, s4_ref, f_ref):
    g = e_ref.shape[0]
    s2 = y_ref.shape[2]
    nw = 2 * LANES
    for j in range(g):
        yy = jnp.concatenate([y_ref[0, j], y_ref[1, j]], axis=0)
        x = _dot(e_ref[j], yy)
        xr = x[:s2].astype(BF16)
        xi = x[s2:].astype(BF16)
        for q in range(FNET_WIDTH // nw):
            f = _dot(xr[:, q * nw:(q + 1) * nw], c4_ref[...]) + _dot(xi[:, q * nw:(q + 1) * nw], s4_ref[...])
            f_ref[:, j * FNET_WIDTH + q * nw:j * FNET_WIDTH + (q + 1) * nw] = f * scale


def _fnet(u, bsz, s_len, tables):
    w1, e, c4, s4 = tables
    s1, s2 = _dft_split(s_len)
    fw = FNET_WIDTH
    ncol = s2 * fw
    nb = min(ncol, 4096)
    u3 = u.reshape(bsz, s1, ncol)
    y = pl.pallas_call(
        _dft1_body,
        out_shape=jax.ShapeDtypeStruct((bsz, 2 * s1, ncol), BF16),
        grid=(bsz, ncol // nb),
        in_specs=[pl.BlockSpec((2 * s1, s1), lambda b, j: (0, 0)),
                  pl.BlockSpec((None, s1, nb), lambda b, j: (b, 0, j))],
        out_specs=pl.BlockSpec((None, 2 * s1, nb), lambda b, j: (b, 0, j)),
        compiler_params=_params("parallel", "parallel"),
        name="dft_stage1",
    )(w1, u3)
    y5 = y.reshape(bsz, 2, s1, s2, fw)
    g = 8
    scale = 1.0 / math.sqrt(s_len * HEAD_DIM)
    f = pl.pallas_call(
        functools.partial(_dft2_body, scale),
        out_shape=jax.ShapeDtypeStruct((bsz, s2, s1 * fw), F32),
        grid=(bsz, s1 // g),
        in_specs=[pl.BlockSpec((g, 2 * s2, 2 * s2), lambda b, j: (j, 0, 0)),
                  pl.BlockSpec((None, 2, g, s2, fw), lambda b, j: (b, 0, j, 0, 0)),
                  pl.BlockSpec(c4.shape, lambda b, j: (0, 0)),
                  pl.BlockSpec(s4.shape, lambda b, j: (0, 0))],
        out_specs=pl.BlockSpec((None, s2, g * fw), lambda b, j: (b, 0, j)),
        compiler_params=_params("parallel", "parallel"),
        name="dft_stage2",
    )(e, y5, c4, s4)
    return f.reshape(bsz * s_len, fw)


def _mix0_out_body(x_ref, f_ref, yf_ref, yb_ref, bonus_ref, gate_ref, lng_ref, lnb_ref, ones_ref, w_ref, g_ref,
                   o_ref):
    ones_bd = ones_ref[...]
    y = yf_ref[...] + yb_ref[...]
    mu = _segsum(y, ones_bd) * (1.0 / HEAD_DIM)
    d = y - mu
    var = _segsum(d * d, ones_bd) * (1.0 / HEAD_DIM)
    yn = d * lax.rsqrt(var + GN_EPS) * lng_ref[...] + lnb_ref[...]
    o = (yn + bonus_ref[...]) * gate_ref[...]
    cat = jnp.concatenate([f_ref[...].astype(BF16), o.astype(BF16)], axis=1)
    h = _dot(cat, w_ref[...])
    o_ref[...] = x_ref[...] + _rms(h, g_ref[...], NORM_EPS)


def _mix0_out(x2d, f, yf, yb, bonus, gate, lng, lnb, ones_bd, w_out, g_post, tm=512):
    t, d = x2d.shape
    w = RWKV_WIDTH
    row = lambda n: pl.BlockSpec((tm, n), lambda i: (i, 0))
    cst = lambda a: pl.BlockSpec(a.shape, lambda i: (0, 0))
    return pl.pallas_call(
        _mix0_out_body,
        out_shape=jax.ShapeDtypeStruct((t, d), F32),
        grid=(t // tm,),
        in_specs=[row(d), row(w), row(w), row(w), row(w), row(w),
                  cst(lng), cst(lnb), cst(ones_bd), cst(w_out), cst(g_post)],
        out_specs=row(d),
        compiler_params=_params("parallel"),
        name="mix0_out",
    )(x2d, f, yf, yb, bonus, gate, lng, lnb, ones_bd, w_out, g_post)


FF_CHUNK = 256


def _ffn_body(x_ref, xp_ref, xn_ref, gpre_ref, wup_ref, cw_ref, cb_ref, wdn_ref, gpost_ref, o_ref,
              h_ref, acc_ref, ua_ref, ub_ref):
    i = pl.program_id(1)
    n = pl.num_programs(1)
    tm = x_ref.shape[0]
    f = FF_CHUNK
    g = gpre_ref[...]
    x = x_ref[...]
    hp = jnp.where(i == 0, 0.0, _rms(xp_ref[...], g, NORM_EPS))
    hn = jnp.where(i == n - 1, 0.0, _rms(xn_ref[...], g, NORM_EPS))
    h_ref[...] = jnp.concatenate([hp, _rms(x, g, NORM_EPS), hn], axis=0).astype(BF16)
    acc_ref[...] = jnp.zeros_like(acc_ref)

    def cols(j):
        return j * f, D_FF + j * f

    def up(j, dst):
        h = h_ref[...]
        for part, c0 in enumerate(cols(j)):
            dst[:, part * f:(part + 1) * f] = _dot(h, wup_ref[:, pl.ds(c0, f)])

    def post(j, src):
        def conv(part, c0):
            u = src[:, part * f:(part + 1) * f]
            cw = cw_ref[:, pl.ds(c0, f)]
            return (cw[0:1] * u[HALO - 1:HALO - 1 + tm] + cw[1:2] * u[HALO:HALO + tm]
                    + cw[2:3] * u[HALO + 1:HALO + 1 + tm] + cb_ref[:, pl.ds(c0, f)])

        cv, cg = cols(j)
        val = conv(0, cv)
        gate = conv(1, cg)
        act = 0.5 * gate * (1.0 + jnp.tanh(math.sqrt(2.0 / math.pi) * (gate + 0.044715 * gate * gate * gate)))
        acc_ref[...] += _dot((act * val).astype(BF16), wdn_ref[pl.ds(cv, f), :])

    n_chunks = D_FF // f
    bufs = (ua_ref, ub_ref)
    up(0, ua_ref)
    for j in range(n_chunks):
        if j + 1 < n_chunks:
            up(j + 1, bufs[(j + 1) % 2])
        post(j, bufs[j % 2])
    o_ref[...] = x + _rms(acc_ref[...], gpost_ref[...], NORM_EPS)


def _ffn(x2d, bsz, s_len, g_pre, w_up, conv_w, conv_b, w_down, g_post, tm=512):
    t, d = x2d.shape
    nb = s_len // tm
    nh = t // HALO
    cst = lambda a: pl.BlockSpec(a.shape, lambda b, i: (0,) * a.ndim, pipeline_mode=pl.Buffered(1))
    return pl.pallas_call(
        _ffn_body,
        out_shape=jax.ShapeDtypeStruct((t, d), F32),
        grid=(bsz, nb),
        in_specs=[pl.BlockSpec((tm, d), lambda b, i: (b * nb + i, 0)),
                  pl.BlockSpec((HALO, d), lambda b, i: (jnp.maximum((b * nb + i) * (tm // HALO) - 1, 0), 0)),
                  pl.BlockSpec((HALO, d), lambda b, i: (jnp.minimum((b * nb + i + 1) * (tm // HALO), nh - 1), 0)),
                  cst(g_pre), cst(w_up), cst(conv_w), cst(conv_b), cst(w_down), cst(g_post)],
        out_specs=pl.BlockSpec((tm, d), lambda b, i: (b * nb + i, 0)),
        scratch_shapes=[pltpu.VMEM((tm + 2 * HALO, d), BF16), pltpu.VMEM((tm, d), F32),
                        pltpu.VMEM((tm + 2 * HALO, 2 * FF_CHUNK), F32),
                        pltpu.VMEM((tm + 2 * HALO, 2 * FF_CHUNK), F32)],
        compiler_params=_params("parallel", "parallel"),
        name="conv_ffn",
    )(x2d, x2d, x2d, g_pre, w_up, conv_w, conv_b, w_down, g_post)


def _qkv_body(x_ref, g_ref, wk_ref, wqt_ref, wvt_ref, ck_ref, sk_ref, cq_ref, sq_ref, k_o, qt_o, vt_o):
    h = _rms(x_ref[...], g_ref[...], NORM_EPS).astype(BF16)
    ck, sk = ck_ref[...], sk_ref[...]
    cq, sq = cq_ref[...], sq_ref[...]
    half = LANES // 2
    for hp in range(DIFF_HEADS // 2):
        kp = _dot(h, wk_ref[:, hp * 2 * LANES:(hp + 1) * 2 * LANES])
        for e in range(2):
            kh = kp[:, e * LANES:(e + 1) * LANES]
            sl = slice((2 * hp + e) * LANES, (2 * hp + e + 1) * LANES)
            k_o[:, sl] = (kh * ck + pltpu.roll(kh, half, 1) * sk).astype(BF16)
    for hd in range(DIFF_HEADS):
        sl = slice(hd * LANES, (hd + 1) * LANES)
        qh = _dot_nt(wqt_ref[sl, :], h)
        qsw = jnp.concatenate([qh[half:], qh[:half]], axis=0)
        qt_o[sl, :] = ((qh * cq + qsw * sq) * (HEAD_DIM ** -0.5 * math.log2(math.e))).astype(BF16)
        vt_o[sl, :] = _dot_nt(wvt_ref[sl, :], h).astype(BF16)


def _qkv(x2d, bsz, s_len, g, wk, wqt, wvt, tabs, tm=512):
    t, d = x2d.shape
    nb = s_len // tm
    ck, sk, cq, sq = tabs
    cst = lambda a: pl.BlockSpec(a.shape, lambda b, i: (0, 0))
    tshape = jax.ShapeDtypeStruct((bsz, d, s_len), BF16)
    return pl.pallas_call(
        _qkv_body,
        out_shape=(jax.ShapeDtypeStruct((t, d), BF16), tshape, tshape),
        grid=(bsz, nb),
        in_specs=[pl.BlockSpec((tm, d), lambda b, i: (b * nb + i, 0)), cst(g), cst(wk), cst(wqt), cst(wvt),
                  pl.BlockSpec((tm, LANES), lambda b, i: (i, 0)),
                  pl.BlockSpec((tm, LANES), lambda b, i: (i, 0)),
                  pl.BlockSpec((LANES, tm), lambda b, i: (0, i)),
                  pl.BlockSpec((LANES, tm), lambda b, i: (0, i))],
        out_specs=(pl.BlockSpec((tm, d), lambda b, i: (b * nb + i, 0)),
                   pl.BlockSpec((None, d, tm), lambda b, i: (b, 0, i)),
                   pl.BlockSpec((None, d, tm), lambda b, i: (b, 0, i))),
        compiler_params=_params("parallel", "parallel"),
        name="qkv_rope",
    )(x2d, g, wk, wqt, wvt, ck, sk, cq, sq)


NEG_BIG = -1e30


def _attn_body(lambda_init, tk, qt_ref, k_ref, vt_ref, lq1, lk1, lq2, lk2, sg_ref, o_ref, sa_ref, sb_ref, acc_ref):
    qt = qt_ref[...]
    tq = qt.shape[1]
    s_len = k_ref.shape[0]
    rowi = lax.broadcasted_iota(jnp.int32, qt.shape, 0)
    first = (rowi // (HEAD_DIM // 2)) % 2 == 0
    zero = jnp.zeros_like(qt)
    qh = (jnp.where(first, qt, zero), jnp.where(first, zero, qt))

    def scores(j, dst):
        kk = k_ref[pl.ds(pl.multiple_of(j * tk, tk), tk), :]
        for c in range(2):
            dst[c] = _dot(kk, qh[c])

    def update(j, src, ml):
        vt = vt_ref[:, pl.ds(pl.multiple_of(j * tk, tk), tk)]
        out = []
        for c in range(2):
            m, l = ml[c]
            s = src[c]
            mn = jnp.maximum(m, jnp.max(s, axis=0, keepdims=True))
            al = jnp.exp2(m - mn)
            p = jnp.exp2(s - mn)
            l = al * l + jnp.sum(p, axis=0, keepdims=True)
            acc_ref[c] = al * acc_ref[c] + _dot(vt, p.astype(BF16))
            out.append((mn, l))
        return tuple(out)

    mi = jnp.full((1, tq), NEG_BIG, F32)
    li = jnp.zeros((1, tq), F32)
    acc_ref[...] = jnp.zeros_like(acc_ref)
    n_tiles = s_len // tk
    def step(i, ml):
        scores(2 * i + 1, sb_ref)
        ml = update(2 * i, sa_ref, ml)
        scores(2 * i + 2, sa_ref)
        return update(2 * i + 1, sb_ref, ml)

    scores(0, sa_ref)
    ml = lax.fori_loop(0, n_tiles // 2 - 1, step, ((mi, li), (mi, li)), unroll=2)
    scores(n_tiles - 1, sb_ref)
    ml = update(n_tiles - 2, sa_ref, ml)
    (m0, l0), (m1, l1) = update(n_tiles - 1, sb_ref, ml)
    a0, a1 = acc_ref[0], acc_ref[1]
    lam = (jnp.exp(jnp.sum(lq1[...] * lk1[...], axis=1, keepdims=True))
           - jnp.exp(jnp.sum(lq2[...] * lk2[...], axis=1, keepdims=True)) + lambda_init)
    o = a0 / l0 - lam * (a1 / l1)
    o = o * lax.rsqrt(jnp.mean(o * o, axis=0, keepdims=True) + SUBLN_EPS) * sg_ref[...] * (1.0 - lambda_init)
    o_ref[...] = o.astype(BF16)


def _attention(qt, k, vt, lq1, lk1, lq2, lk2, sg, lambda_init, bsz, s_len, tq=256, tk=512):
    d = qt.shape[1]
    assert s_len % (2 * tk) == 0 and s_len % tq == 0, (s_len, tq, tk)
    k3 = k.reshape(bsz, s_len, d)
    cst = lambda a: pl.BlockSpec(a.shape, lambda b, h, i: (0, 0))
    return pl.pallas_call(
        functools.partial(_attn_body, lambda_init, tk),
        out_shape=jax.ShapeDtypeStruct((bsz, d, s_len), BF16),
        grid=(bsz, DIFF_HEADS, s_len // tq),
        in_specs=[pl.BlockSpec((None, LANES, tq), lambda b, h, i: (b, h, i)),
                  pl.BlockSpec((None, s_len, LANES), lambda b, h, i: (b, 0, h)),
                  pl.BlockSpec((None, LANES, s_len), lambda b, h, i: (b, h, 0)),
                  cst(lq1), cst(lk1), cst(lq2), cst(lk2), cst(sg)],
        out_specs=pl.BlockSpec((None, LANES, tq), lambda b, h, i: (b, h, i)),
        scratch_shapes=[pltpu.VMEM((2, tk, tq), F32), pltpu.VMEM((2, tk, tq), F32),
                        pltpu.VMEM((2, LANES, tq), F32)],
        compiler_params=_params("parallel", "parallel", "parallel"),
        name="diff_attn",
    )(qt, k3, vt, lq1, lk1, lq2, lk2, sg)


def _attn_out_body(x_ref, ot_ref, w_ref, g_ref, o_ref):
    h = _dot_tn(ot_ref[...], w_ref[...])
    o_ref[...] = x_ref[...] + _rms(h, g_ref[...], NORM_EPS)


def _attn_out(x2d, ot, w_o, g_post, bsz, s_len, tm=512):
    t, d = x2d.shape
    nb = s_len // tm
    cst = lambda a: pl.BlockSpec(a.shape, lambda b, i: (0, 0))
    return pl.pallas_call(
        _attn_out_body,
        out_shape=jax.ShapeDtypeStruct((t, d), F32),
        grid=(bsz, nb),
        in_specs=[pl.BlockSpec((tm, d), lambda b, i: (b * nb + i, 0)),
                  pl.BlockSpec((None, d, tm), lambda b, i: (b, 0, i)), cst(w_o), cst(g_post)],
        out_specs=pl.BlockSpec((tm, d), lambda b, i: (b * nb + i, 0)),
        compiler_params=_params("parallel", "parallel"),
        name="attn_out",
    )(x2d, ot, w_o, g_post)


def _block_diag2(a):
    z = jnp.zeros_like(a[0])
    return jnp.concatenate([jnp.concatenate([a[0], z], axis=1), jnp.concatenate([z, a[1]], axis=1)], axis=0)


def _rope_perm():
    half = HEAD_DIM // 2
    idx = []
    for h in range(DIFF_HEADS):
        for part in range(2):
            for c in range(2):
                idx.extend(h * 2 * HEAD_DIM + c * HEAD_DIM + part * half + np.arange(half))
    return np.asarray(idx)


def _rope_tables(s_len):
    half = HEAD_DIM // 2
    inv_freq = ROPE_THETA ** (-jnp.arange(half, dtype=F32) / half)
    ang = jnp.arange(s_len, dtype=F32)[:, None] * inv_freq[None, :]
    cos, sin = jnp.cos(ang), jnp.sin(ang)
    ck = jnp.concatenate([cos] * 4, axis=1)
    sk = jnp.concatenate([-sin, -sin, sin, sin], axis=1)
    return ck, sk, ck.T, sk.T


def kernel(x_prompt, x_sample, mix0_norm_pre, mix0_norm_post, w_in0, mu_prev, mu_next, decay_w0, decay_w2, iclr_a0, iclr_a2, gate_g2, k_k, k_a, r_k, lnx_g, lnx_b, w_out0, mix1_norm_pre, mix1_norm_post, w_qkv1, lambda_q1, lambda_k1, lambda_q2, lambda_k2, subln_g, w_o1, ffn_norm_pre, ffn_norm_post, w_up, conv_w, conv_b, w_down):
    w = RWKV_WIDTH
    ones_bd = jnp.asarray(np.kron(np.eye(w // HEAD_DIM), np.ones((HEAD_DIM, HEAD_DIM))), F32).astype(BF16)
    row = lambda a: a.reshape(1, -1)

    w_in_bf = w_in0[0].astype(BF16)
    prep_consts = (row(mu_prev[0]), row(mu_next[0]),
                   row(decay_w0[0]), _block_diag2(decay_w2[0]).astype(BF16),
                   row(iclr_a0[0]), _block_diag2(iclr_a2[0]).astype(BF16),
                   gate_g2[0].astype(BF16), row(k_k[0]), row(k_a[0]), row(r_k[0]), ones_bd)
    w_out_bf = w_out0[0].astype(BF16)

    perm = _rope_perm()
    dq = DIFF_HEADS * 2 * HEAD_DIM
    wq = w_qkv1[0][:, :dq][:, perm]
    wk = w_qkv1[0][:, dq:2 * dq][:, perm]
    wv = w_qkv1[0][:, 2 * dq:]
    wk_bf, wqt_bf, wvt_bf = wk.astype(BF16), wq.T.astype(BF16), wv.T.astype(BF16)
    w_o_bf = w_o1[0].astype(BF16)
    lambda_init = 0.8 - 0.6 * math.exp(-0.3 * 1)
    sg_col = subln_g[0].reshape(-1, 1)

    w_up_bf = w_up.astype(BF16)
    w_down_bf = w_down.astype(BF16)

    def ffn(x2d, bsz, s_len, layer):
        return _ffn(x2d, bsz, s_len, row(ffn_norm_pre[layer]), w_up_bf[layer], conv_w[layer],
                    row(conv_b[layer]), w_down_bf[layer], row(ffn_norm_post[layer]))

    def trunk(x):
        bsz, s_len, d = x.shape
        x2d = x.reshape(bsz * s_len, d)
        u, z = _in_proj(x2d, mix0_norm_pre[0], w_in_bf)
        f = _fnet(u, bsz, s_len, _dft_tables(s_len))
        r, v, kn, lwf, lwb, kf, kb, bf, bb, gate, bonus = _rwkv_prep(z, bsz, s_len, prep_consts)
        yf, yb = _wkv_scan(r, v, kn, lwf, lwb, kf, kb, bf, bb, bsz, s_len)
        x2d = _mix0_out(x2d, f, yf, yb, bonus, gate, row(lnx_g[0]), row(lnx_b[0]), ones_bd, w_out_bf,
                        row(mix0_norm_post[0]))
        x2d = ffn(x2d, bsz, s_len, 0)
        k, qt, vt = _qkv(x2d, bsz, s_len, row(mix1_norm_pre[0]), wk_bf, wqt_bf, wvt_bf, _rope_tables(s_len))
        ot = _attention(qt, k, vt, row(lambda_q1[0]), row(lambda_k1[0]), row(lambda_q2[0]), row(lambda_k2[0]),
                        sg_col, lambda_init, bsz, s_len)
        x2d = _attn_out(x2d, ot, w_o_bf, row(mix1_norm_post[0]), bsz, s_len)
        x2d = ffn(x2d, bsz, s_len, 1)
        return x2d.reshape(bsz, s_len, d)

    return (trunk(x_prompt), trunk(x_sample))
```

```python
import functools
import math

import numpy as np
import jax
import jax.numpy as jnp
from jax import lax
from jax.experimental import pallas as pl
from jax.experimental.pallas import tpu as pltpu

F32 = jnp.float32
BF16 = jnp.bfloat16

D_MODEL = 1024
HEAD_DIM = 64
FNET_WIDTH = 512
RWKV_WIDTH = 512
RWKV_HEADS = 8
RWKV_IN = 1920
D_FF = 2816
NORM_EPS = 1e-6
GN_EPS = 64e-5
SUBLN_EPS = 1e-5
ROPE_THETA = 10000.0
DIFF_HEADS = 8
LANES = 128
HALO = 8
CHUNK = 64
WKV_CHUNKS_PER_STEP = 2
RWKV_PREP_ROWS = 256
VMEM_LIMIT = 56 * 1024 * 1024
HIGHEST = lax.Precision.HIGHEST


def _params(*sem):
    return pltpu.CompilerParams(dimension_semantics=sem, vmem_limit_bytes=VMEM_LIMIT)


def _rms(x, g, eps):
    return x * lax.rsqrt(jnp.mean(x * x, axis=-1, keepdims=True) + eps) * g


def _dot(a, b):
    return jnp.dot(a, b, preferred_element_type=F32)


def _dot_nt(a, b):
    return lax.dot_general(a, b, (((1,), (1,)), ((), ())), preferred_element_type=F32)


def _dot_tn(a, b):
    return lax.dot_general(a, b, (((0,), (0,)), ((), ())), preferred_element_type=F32)


def _sigmoid(x):
    return 1.0 / (1.0 + jnp.exp(-x))


def _dot_split3(m_exact, x):
    h1 = x.astype(BF16)
    r1 = x - h1.astype(F32)
    h2 = r1.astype(BF16)
    h3 = (r1 - h2.astype(F32)).astype(BF16)
    return _dot(m_exact, h1) + _dot(m_exact, h2) + _dot(m_exact, h3)


def _segsum(x, ones_bd):
    hi = x.astype(BF16)
    lo = (x - hi.astype(F32)).astype(BF16)
    return _dot(hi, ones_bd) + _dot(lo, ones_bd)


def _in_proj_body(x_ref, g_ref, w_ref, u_ref, z_ref):
    h = _rms(x_ref[...], g_ref[...], NORM_EPS).astype(BF16)
    y = _dot(h, w_ref[...])
    u_ref[...] = y[:, :FNET_WIDTH].astype(BF16)
    z_ref[...] = y[:, FNET_WIDTH:]


def _in_proj(x2d, g, w, tm=512):
    t, d = x2d.shape
    n = w.shape[1]
    return pl.pallas_call(
        _in_proj_body,
        out_shape=(jax.ShapeDtypeStruct((t, FNET_WIDTH), BF16),
                   jax.ShapeDtypeStruct((t, n - FNET_WIDTH), F32)),
        grid=(t // tm,),
        in_specs=[pl.BlockSpec((tm, d), lambda i: (i, 0)),
                  pl.BlockSpec((1, d), lambda i: (0, 0)),
                  pl.BlockSpec((d, n), lambda i: (0, 0))],
        out_specs=(pl.BlockSpec((tm, FNET_WIDTH), lambda i: (i, 0)),
                   pl.BlockSpec((tm, n - FNET_WIDTH), lambda i: (i, 0))),
        compiler_params=_params("parallel"),
        name="in_proj",
    )(x2d, g.reshape(1, d), w)


def _rwkv_prep_body(z_ref, zp_ref, zn_ref, mup_ref, mun_ref, w0_ref, w2_ref, a0_ref, a2_ref, g2_ref,
                    kk_ref, ka_ref, rk_ref, ones_ref, trif_ref, trib_ref,
                    r_o, v_o, kn_o, lwf_o, lwb_o, kf_o, kb_o, bf_o, bb_o, gate_o, bonus_o):
    i = pl.program_id(1)
    n = pl.num_programs(1)
    z = z_ref[...]
    tm = z.shape[0]
    rows = lax.broadcasted_iota(jnp.int32, z.shape, 0)
    prev_row = jnp.where(i == 0, 0.0, zp_ref[HALO - 1:HALO, :])
    next_row = jnp.where(i == n - 1, 0.0, zn_ref[0:1, :])
    z_prev = jnp.where(rows == 0, prev_row, pltpu.roll(z, 1, 0))
    z_next = jnp.where(rows == tm - 1, next_row, pltpu.roll(z, tm - 1, 0))
    zs = z + mup_ref[...] * (z_prev - z) + mun_ref[...] * (z_next - z)

    w = RWKV_WIDTH
    r = zs[:, 0:w]
    k = zs[:, w:2 * w]
    v = zs[:, 2 * w:3 * w]
    wd = zs[:, 3 * w:3 * w + LANES]
    ad = zs[:, 3 * w + LANES:3 * w + 2 * LANES]
    gd = zs[:, 3 * w + 2 * LANES:3 * w + 3 * LANES]

    yw = w0_ref[...] + _dot(jnp.tanh(wd).astype(BF16), w2_ref[...])
    lw = -math.exp(-0.5) * _sigmoid(yw)
    iclr = _sigmoid(a0_ref[...] + _dot(ad.astype(BF16), a2_ref[...]))
    gate = _dot(_sigmoid(gd).astype(BF16), g2_ref[...])

    ones_bd = ones_ref[...]
    kk = k * kk_ref[...]
    kn = kk / jnp.maximum(jnp.sqrt(_segsum(kk * kk, ones_bd)), 1e-12)
    ka = ka_ref[...]
    k_f = k * (1.0 + (iclr[:, :w] - 1.0) * ka)
    k_b = k * (1.0 + (iclr[:, w:] - 1.0) * ka)
    bonus = _segsum(r * rk_ref[...] * (k_f + k_b), ones_bd)

    r_o[...] = r
    v_o[...] = v
    kn_o[...] = kn
    lwf_o[...] = _dot_split3(trif_ref[...], lw[:, :w])
    lwb_o[...] = _dot_split3(trib_ref[...], lw[:, w:])
    kf_o[...] = k_f
    kb_o[...] = k_b
    bf_o[...] = kn * iclr[:, :w]
    bb_o[...] = kn * iclr[:, w:]
    gate_o[...] = gate
    bonus_o[...] = bonus * v


def _rwkv_prep(z, bsz, s_len, consts, tm=RWKV_PREP_ROWS):
    t, n = z.shape
    nb = s_len // tm
    nh = t // HALO
    w = RWKV_WIDTH

    def full(a):
        return pl.BlockSpec(a.shape, lambda b, i: (0,) * a.ndim)

    out = jax.ShapeDtypeStruct((t, w), F32)
    return pl.pallas_call(
        _rwkv_prep_body,
        out_shape=(out,) * 11,
        grid=(bsz, nb),
        in_specs=[pl.BlockSpec((tm, n), lambda b, i: (b * nb + i, 0)),
                  pl.BlockSpec((HALO, n), lambda b, i: (jnp.maximum((b * nb + i) * (tm // HALO) - 1, 0), 0)),
                  pl.BlockSpec((HALO, n), lambda b, i: (jnp.minimum((b * nb + i + 1) * (tm // HALO), nh - 1), 0)),
                  ] + [full(a) for a in consts],
        out_specs=(pl.BlockSpec((tm, w), lambda b, i: (b * nb + i, 0)),) * 11,
        compiler_params=_params("parallel", "parallel"),
        name="rwkv_prep",
    )(z, z, z, *consts)


def _wkv_items(reverse, r_ref, v_ref, kn_ref, cum_ref, k_ref, b_ref, y_ref, ht_ref):
    c = CHUNK
    half = LANES // 2
    row2 = lax.broadcasted_iota(jnp.int32, (2 * c, LANES), 0)
    lane2 = lax.broadcasted_iota(jnp.int32, (2 * c, LANES), 1)
    t_row, t_col = row2 % c, lane2 % c
    own = (row2 // c) == (lane2 // HEAD_DIM)
    first = row2 < c
    order = (t_row < t_col, t_row <= t_col) if reverse else (t_row > t_col, t_row >= t_col)
    strict, incl = own & order[0], own & order[1]
    lane = lax.broadcasted_iota(jnp.int32, (c, LANES), 1)
    m0 = lane < HEAD_DIM

    def stack(x):
        return jnp.concatenate([jnp.where(m0, x, 0.0), jnp.where(m0, 0.0, x)], axis=0)

    trow = lax.broadcasted_iota(jnp.int32, (c, LANES), 0)
    items = []
    subs = range(WKV_CHUNKS_PER_STEP)
    for sub, p in [(s, p) for s in (reversed(subs) if reverse else subs) for p in range(RWKV_WIDTH // LANES)]:
        rows = slice(sub * c, (sub + 1) * c)
        sl = slice(p * LANES, (p + 1) * LANES)
        cum = cum_ref[rows, sl]
        if reverse:
            cum_ex = jnp.where(trow == c - 1, 0.0, pltpu.roll(cum, c - 1, 0))
            tot = cum[0:1]
        else:
            cum_ex = jnp.where(trow == 0, 0.0, pltpu.roll(cum, 1, 0))
            tot = cum[c - 1:c]
        r, v, kn, k, b = r_ref[rows, sl], v_ref[rows, sl], kn_ref[rows, sl], k_ref[rows, sl], b_ref[rows, sl]
        e_neg = jnp.exp(-cum)
        e_end = jnp.exp(tot - cum)
        am2 = stack(-kn * jnp.exp(cum_ex))
        rm2 = stack(r * jnp.exp(cum))
        vm2 = stack(v).astype(BF16)
        rhs = jnp.concatenate([b * e_neg, k * e_neg], axis=0).astype(BF16)
        aa = _dot_nt(jnp.concatenate([am2, rm2], axis=0).astype(BF16), rhs)
        ar = pltpu.roll(aa, half, 1)
        aa_a, ar_a, aa_r, ar_r = aa[:2 * c], ar[:2 * c], aa[2 * c:], ar[2 * c:]
        items.append(dict(
            p=p, rows=rows, sl=sl, y_ref=y_ref, ht_ref=ht_ref, own=own, v=v, am2=am2, rm2=rm2, vm2=vm2,
            etot=jnp.exp(tot),
            bk=jnp.concatenate([b * e_end, k * e_end], axis=0).astype(BF16),
            abd_b=jnp.where(strict, jnp.where(first, aa_a, ar_a), 0.0),
            abd_k=jnp.where(strict, jnp.where(first, ar_a, aa_a), 0.0).astype(BF16),
            rbd=jnp.concatenate([jnp.where(incl, jnp.where(first, aa_r, ar_r), 0.0),
                                 jnp.where(incl, jnp.where(first, ar_r, aa_r), 0.0)], axis=1).astype(BF16)))
    return items


def _wkv_solve(items):
    half = LANES // 2
    for it in items:
        it["x"] = it["am2"] + pltpu.roll(_dot(it["abd_k"], it["vm2"]), half, 1)
        it["pw"] = it["abd_b"]
    n_lvl = int(math.log2(CHUNK))
    for lvl in range(n_lvl):
        for it in items:
            x, pw = it["x"], it["pw"].astype(BF16)
            if lvl < n_lvl - 1:
                pr = _dot(pw, jnp.concatenate([x, it["pw"]], axis=1).astype(BF16))
                it["x"] = x + pr[:, :LANES]
                it["pw"] = pr[:, LANES:]
            else:
                it["x"] = x + _dot(pw, x.astype(BF16))


def _wkv_finish(items):
    c = CHUNK
    half = LANES // 2
    r128 = lax.broadcasted_iota(jnp.int32, (LANES, LANES), 0)
    c128 = lax.broadcasted_iota(jnp.int32, (LANES, LANES), 1)
    same_head = (r128 // HEAD_DIM) == (c128 // HEAD_DIM)
    for it in items:
        x, own, rbd, bk = it["x"], it["own"], it["rbd"], it["bk"]
        p1_2 = jnp.where(own, x, 0.0)
        p2_2 = jnp.where(own, pltpu.roll(x, half, 1), 0.0)
        q1_2 = it["rm2"] + _dot(rbd[:, :LANES], p1_2.astype(BF16))
        q2_2 = _dot(rbd, jnp.concatenate([p2_2.astype(BF16), it["vm2"]], axis=0))
        p1, p2 = p1_2[:c] + p1_2[c:], p2_2[:c] + p2_2[c:]
        q1, q2 = q1_2[:c] + q1_2[c:], q2_2[:c] + q2_2[c:]
        bp = jnp.where(same_head, _dot_tn(bk[:c], p1.astype(BF16)), 0.0)
        gt = jnp.where(same_head, _dot_tn(jnp.concatenate([p2, it["v"]], axis=0).astype(BF16), bk), 0.0)
        ht = it["ht_ref"][it["p"]]
        hb = ht.astype(BF16)
        it["y_ref"][it["rows"], it["sl"]] = _dot_nt(q1.astype(BF16), hb) + q2
        it["ht_ref"][it["p"]] = it["etot"] * ht + _dot_nt(hb, bp.astype(BF16)) + gt


def _wkv_body(rf, vf, knf, cumf, kf, bf, rb, vb, knb, cumb, kb, bb, yf, yb, hf_ref, hb_ref):
    @pl.when(pl.program_id(1) == 0)
    def _():
        hf_ref[...] = jnp.zeros_like(hf_ref)
        hb_ref[...] = jnp.zeros_like(hb_ref)

    items = (_wkv_items(False, rf, vf, knf, cumf, kf, bf, yf, hf_ref)
             + _wkv_items(True, rb, vb, knb, cumb, kb, bb, yb, hb_ref))
    _wkv_solve(items)
    _wkv_finish(items)


def _wkv_scan(r, v, kn, cumf, cumb, kf, kb, bf, bb, bsz, s_len):
    t, w = r.shape
    c = CHUNK * WKV_CHUNKS_PER_STEP
    nc = s_len // c
    fwd = pl.BlockSpec((c, w), lambda b, i: (b * nc + i, 0))
    bwd = pl.BlockSpec((c, w), lambda b, i: (b * nc + nc - 1 - i, 0))
    out = jax.ShapeDtypeStruct((t, w), F32)
    n_pairs = w // LANES
    return pl.pallas_call(
        _wkv_body,
        out_shape=(out, out),
        grid=(bsz, nc),
        in_specs=[fwd] * 6 + [bwd] * 6,
        out_specs=(fwd, bwd),
        scratch_shapes=[pltpu.VMEM((n_pairs, LANES, LANES), F32),
                        pltpu.VMEM((n_pairs, LANES, LANES), F32)],
        compiler_params=_params("parallel", "arbitrary"),
        name="wkv_scan",
    )(r, v, kn, cumf, kf, bf, r, v, kn, cumb, kb, bb)


def _dft_split(s_len):
    s1 = 1 << ((s_len.bit_length() - 1 + 1) // 2)
    return s1, s_len // s1


def _dft_tables(s_len):
    s1, s2 = _dft_split(s_len)

    def cs(num, period):
        ang = (num % period).astype(F32) * (2.0 * math.pi / period)
        return jnp.cos(ang), jnp.sin(ang)

    k1 = jnp.arange(s1, dtype=jnp.int32)
    c1, sn1 = cs(k1[:, None] * k1[None, :], s1)
    w1 = jnp.concatenate([c1, sn1], axis=0)
    k = k1[:, None, None] + s1 * jnp.arange(s2, dtype=jnp.int32)[None, :, None]
    ec, es = cs(k * jnp.arange(s2, dtype=jnp.int32)[None, None, :], s_len)
    e = jnp.concatenate([jnp.concatenate([ec, -es], axis=2),
                         jnp.concatenate([-es, -ec], axis=2)], axis=1)
    c = jnp.arange(HEAD_DIM, dtype=jnp.int32)
    c3, s3 = cs(c[:, None] * c[None, :], HEAD_DIM)
    eye = jnp.eye(2 * LANES // HEAD_DIM, dtype=F32)
    return (w1.astype(BF16), e.astype(BF16), jnp.kron(eye, c3).astype(BF16), jnp.kron(eye, s3).astype(BF16))


def _dft1_body(w_ref, u_ref, y_ref):
    y_ref[...] = _dot(w_ref[...], u_ref[...]).astype(BF16)


def _dft2_body(scale, e_ref, y_ref, c4_ref, s4_ref, f_ref):
    g = e_ref.shape[0]
    s2 = y_ref.shape[2]
    nw = 2 * LANES
    for j in range(g):
        yy = jnp.concatenate([y_ref[0, j], y_ref[1, j]], axis=0)
        x = _dot(e_ref[j], yy)
        xr = x[:s2].astype(BF16)
        xi = x[s2:].astype(BF16)
        for q in range(FNET_WIDTH // nw):
            f = _dot(xr[:, q * nw:(q + 1) * nw], c4_ref[...]) + _dot(xi[:, q * nw:(q + 1) * nw], s4_ref[...])
            f_ref[:, j * FNET_WIDTH + q * nw:j * FNET_WIDTH + (q + 1) * nw] = f * scale


def _fnet(u, bsz, s_len, tables):
    w1, e, c4, s4 = tables
    s1, s2 = _dft_split(s_len)
    fw = FNET_WIDTH
    ncol = s2 * fw
    nb = min(ncol, 4096)
    u3 = u.reshape(bsz, s1, ncol)
    y = pl.pallas_call(
        _dft1_body,
        out_shape=jax.ShapeDtypeStruct((bsz, 2 * s1, ncol), BF16),
        grid=(bsz, ncol // nb),
        in_specs=[pl.BlockSpec((2 * s1, s1), lambda b, j: (0, 0)),
                  pl.BlockSpec((None, s1, nb), lambda b, j: (b, 0, j))],
        out_specs=pl.BlockSpec((None, 2 * s1, nb), lambda b, j: (b, 0, j)),
        compiler_params=_params("parallel", "parallel"),
        name="dft_stage1",
    )(w1, u3)
    y5 = y.reshape(bsz, 2, s1, s2, fw)
    g = 8
    scale = 1.0 / math.sqrt(s_len * HEAD_DIM)
    f = pl.pallas_call(
        functools.partial(_dft2_body, scale),
        out_shape=jax.ShapeDtypeStruct((bsz, s2, s1 * fw), F32),
        grid=(bsz, s1 // g),
        in_specs=[pl.BlockSpec((g, 2 * s2, 2 * s2), lambda b, j: (j, 0, 0)),
                  pl.BlockSpec((None, 2, g, s2, fw), lambda b, j: (b, 0, j, 0, 0)),
                  pl.BlockSpec(c4.shape, lambda b, j: (0, 0)),
                  pl.BlockSpec(s4.shape, lambda b, j: (0, 0))],
        out_specs=pl.BlockSpec((None, s2, g * fw), lambda b, j: (b, 0, j)),
        compiler_params=_params("parallel", "parallel"),
        name="dft_stage2",
    )(e, y5, c4, s4)
    return f.reshape(bsz * s_len, fw)


def _mix0_out_body(x_ref, f_ref, yf_ref, yb_ref, bonus_ref, gate_ref, lng_ref, lnb_ref, ones_ref, w_ref, g_ref,
                   o_ref):
    ones_bd = ones_ref[...]
    y = yf_ref[...] + yb_ref[...]
    mu = _segsum(y, ones_bd) * (1.0 / HEAD_DIM)
    d = y - mu
    var = _segsum(d * d, ones_bd) * (1.0 / HEAD_DIM)
    yn = d * lax.rsqrt(var + GN_EPS) * lng_ref[...] + lnb_ref[...]
    o = (yn + bonus_ref[...]) * gate_ref[...]
    cat = jnp.concatenate([f_ref[...].astype(BF16), o.astype(BF16)], axis=1)
    h = _dot(cat, w_ref[...])
    o_ref[...] = x_ref[...] + _rms(h, g_ref[...], NORM_EPS)


def _mix0_out(x2d, f, yf, yb, bonus, gate, lng, lnb, ones_bd, w_out, g_post, tm=512):
    t, d = x2d.shape
    w = RWKV_WIDTH
    row = lambda n: pl.BlockSpec((tm, n), lambda i: (i, 0))
    cst = lambda a: pl.BlockSpec(a.shape, lambda i: (0, 0))
    return pl.pallas_call(
        _mix0_out_body,
        out_shape=jax.ShapeDtypeStruct((t, d), F32),
        grid=(t // tm,),
        in_specs=[row(d), row(w), row(w), row(w), row(w), row(w),
                  cst(lng), cst(lnb), cst(ones_bd), cst(w_out), cst(g_post)],
        out_specs=row(d),
        compiler_params=_params("parallel"),
        name="mix0_out",
    )(x2d, f, yf, yb, bonus, gate, lng, lnb, ones_bd, w_out, g_post)


FF_CHUNK = 256


def _ffn_body(x_ref, xp_ref, xn_ref, gpre_ref, wup_ref, cw_ref, cb_ref, wdn_ref, gpost_ref, o_ref,
              h_ref, acc_ref, ua_ref, ub_ref):
    i = pl.program_id(1)
    n = pl.num_programs(1)
    tm = x_ref.shape[0]
    f = FF_CHUNK
    g = gpre_ref[...]
    x = x_ref[...]
    hp = jnp.where(i == 0, 0.0, _rms(xp_ref[...], g, NORM_EPS))
    hn = jnp.where(i == n - 1, 0.0, _rms(xn_ref[...], g, NORM_EPS))
    h_ref[...] = jnp.concatenate([hp, _rms(x, g, NORM_EPS), hn], axis=0).astype(BF16)
    acc_ref[...] = jnp.zeros_like(acc_ref)

    def cols(j):
        return j * f, D_FF + j * f

    def up(j, dst):
        h = h_ref[...]
        for part, c0 in enumerate(cols(j)):
            dst[:, part * f:(part + 1) * f] = _dot(h, wup_ref[:, pl.ds(c0, f)])

    def post(j, src):
        def conv(part, c0):
            u = src[:, part * f:(part + 1) * f]
            cw = cw_ref[:, pl.ds(c0, f)]
            return (cw[0:1] * u[HALO - 1:HALO - 1 + tm] + cw[1:2] * u[HALO:HALO + tm]
                    + cw[2:3] * u[HALO + 1:HALO + 1 + tm] + cb_ref[:, pl.ds(c0, f)])

        cv, cg = cols(j)
        val = conv(0, cv)
        gate = conv(1, cg)
        act = 0.5 * gate * (1.0 + jnp.tanh(math.sqrt(2.0 / math.pi) * (gate + 0.044715 * gate * gate * gate)))
        acc_ref[...] += _dot((act * val).astype(BF16), wdn_ref[pl.ds(cv, f), :])

    n_chunks = D_FF // f
    bufs = (ua_ref, ub_ref)
    up(0, ua_ref)
    for j in range(n_chunks):
        if j + 1 < n_chunks:
            up(j + 1, bufs[(j + 1) % 2])
        post(j, bufs[j % 2])
    o_ref[...] = x + _rms(acc_ref[...], gpost_ref[...], NORM_EPS)


def _ffn(x2d, bsz, s_len, g_pre, w_up, conv_w, conv_b, w_down, g_post, tm=512):
    t, d = x2d.shape
    nb = s_len // tm
    nh = t // HALO
    cst = lambda a: pl.BlockSpec(a.shape, lambda b, i: (0,) * a.ndim, pipeline_mode=pl.Buffered(1))
    return pl.pallas_call(
        _ffn_body,
        out_shape=jax.ShapeDtypeStruct((t, d), F32),
        grid=(bsz, nb),
        in_specs=[pl.BlockSpec((tm, d), lambda b, i: (b * nb + i, 0)),
                  pl.BlockSpec((HALO, d), lambda b, i: (jnp.maximum((b * nb + i) * (tm // HALO) - 1, 0), 0)),
                  pl.BlockSpec((HALO, d), lambda b, i: (jnp.minimum((b * nb + i + 1) * (tm // HALO), nh - 1), 0)),
                  cst(g_pre), cst(w_up), cst(conv_w), cst(conv_b), cst(w_down), cst(g_post)],
        out_specs=pl.BlockSpec((tm, d), lambda b, i: (b * nb + i, 0)),
        scratch_shapes=[pltpu.VMEM((tm + 2 * HALO, d), BF16), pltpu.VMEM((tm, d), F32),
                        pltpu.VMEM((tm + 2 * HALO, 2 * FF_CHUNK), F32),
                        pltpu.VMEM((tm + 2 * HALO, 2 * FF_CHUNK), F32)],
        compiler_params=_params("parallel", "parallel"),
        name="conv_ffn",
    )(x2d, x2d, x2d, g_pre, w_up, conv_w, conv_b, w_down, g_post)


def _qkv_body(x_ref, g_ref, wk_ref, wqt_ref, wvt_ref, ck_ref, sk_ref, cq_ref, sq_ref, k_o, qt_o, vt_o):
    h = _rms(x_ref[...], g_ref[...], NORM_EPS).astype(BF16)
    ck, sk = ck_ref[...], sk_ref[...]
    cq, sq = cq_ref[...], sq_ref[...]
    half = LANES // 2
    for hp in range(DIFF_HEADS // 2):
        kp = _dot(h, wk_ref[:, hp * 2 * LANES:(hp + 1) * 2 * LANES])
        for e in range(2):
            kh = kp[:, e * LANES:(e + 1) * LANES]
            sl = slice((2 * hp + e) * LANES, (2 * hp + e + 1) * LANES)
            k_o[:, sl] = (kh * ck + pltpu.roll(kh, half, 1) * sk).astype(BF16)
    for hd in range(DIFF_HEADS):
        sl = slice(hd * LANES, (hd + 1) * LANES)
        qh = _dot_nt(wqt_ref[sl, :], h)
        qsw = jnp.concatenate([qh[half:], qh[:half]], axis=0)
        qt_o[sl, :] = ((qh * cq + qsw * sq) * (HEAD_DIM ** -0.5 * math.log2(math.e))).astype(BF16)
        vt_o[sl, :] = _dot_nt(wvt_ref[sl, :], h).astype(BF16)


def _qkv(x2d, bsz, s_len, g, wk, wqt, wvt, tabs, tm=512):
    t, d = x2d.shape
    nb = s_len // tm
    ck, sk, cq, sq = tabs
    cst = lambda a: pl.BlockSpec(a.shape, lambda b, i: (0, 0))
    tshape = jax.ShapeDtypeStruct((bsz, d, s_len), BF16)
    return pl.pallas_call(
        _qkv_body,
        out_shape=(jax.ShapeDtypeStruct((t, d), BF16), tshape, tshape),
        grid=(bsz, nb),
        in_specs=[pl.BlockSpec((tm, d), lambda b, i: (b * nb + i, 0)), cst(g), cst(wk), cst(wqt), cst(wvt),
                  pl.BlockSpec((tm, LANES), lambda b, i: (i, 0)),
                  pl.BlockSpec((tm, LANES), lambda b, i: (i, 0)),
                  pl.BlockSpec((LANES, tm), lambda b, i: (0, i)),
                  pl.BlockSpec((LANES, tm), lambda b, i: (0, i))],
        out_specs=(pl.BlockSpec((tm, d), lambda b, i: (b * nb + i, 0)),
                   pl.BlockSpec((None, d, tm), lambda b, i: (b, 0, i)),
                   pl.BlockSpec((None, d, tm), lambda b, i: (b, 0, i))),
        compiler_params=_params("parallel", "parallel"),
        name="qkv_rope",
    )(x2d, g, wk, wqt, wvt, ck, sk, cq, sq)


NEG_BIG = -1e30
ATTN_ONES_ROWS = 16


def _attn_body(lambda_init, tk, qt_ref, k_ref, vt_ref, lq1, lk1, lq2, lk2, sg_ref, o_ref, sa_ref, sb_ref, acc_ref):
    qt = qt_ref[...]
    tq = qt.shape[1]
    s_len = k_ref.shape[0]
    rowi = lax.broadcasted_iota(jnp.int32, qt.shape, 0)
    first = (rowi // (HEAD_DIM // 2)) % 2 == 0
    zero = jnp.zeros_like(qt)
    qh = (jnp.where(first, qt, zero), jnp.where(first, zero, qt))

    def scores(j, dst):
        kk = k_ref[pl.ds(pl.multiple_of(j * tk, tk), tk), :]
        for c in range(2):
            dst[c] = _dot(kk, qh[c])

    ones_row = (lax.broadcasted_iota(jnp.int32, (ATTN_ONES_ROWS, tk), 0) == 0).astype(BF16)

    def update(j, src, ms):
        vt = vt_ref[:, pl.ds(pl.multiple_of(j * tk, tk), tk)]
        vte = jnp.concatenate([vt, ones_row], axis=0)
        out = []
        for c in range(2):
            s = src[c]
            mn = jnp.maximum(ms[c], jnp.max(s, axis=0, keepdims=True))
            p = jnp.exp2(s - mn).astype(BF16)
            acc_ref[c] = jnp.exp2(ms[c] - mn) * acc_ref[c] + _dot(vte, p)
            out.append(mn)
        return tuple(out)

    mi = jnp.full((1, tq), NEG_BIG, F32)
    acc_ref[...] = jnp.zeros_like(acc_ref)
    n_tiles = s_len // tk

    def step(i, ms):
        scores(2 * i + 1, sb_ref)
        ms = update(2 * i, sa_ref, ms)
        scores(2 * i + 2, sa_ref)
        return update(2 * i + 1, sb_ref, ms)

    scores(0, sa_ref)
    ms = lax.fori_loop(0, n_tiles // 2 - 1, step, (mi, mi), unroll=2)
    scores(n_tiles - 1, sb_ref)
    ms = update(n_tiles - 2, sa_ref, ms)
    update(n_tiles - 1, sb_ref, ms)
    a0, a1 = acc_ref[0, :LANES], acc_ref[1, :LANES]
    l0, l1 = acc_ref[0, LANES:LANES + 1], acc_ref[1, LANES:LANES + 1]
    lam = (jnp.exp(jnp.sum(lq1[...] * lk1[...], axis=1, keepdims=True))
           - jnp.exp(jnp.sum(lq2[...] * lk2[...], axis=1, keepdims=True)) + lambda_init)
    o = a0 / l0 - lam * (a1 / l1)
    o = o * lax.rsqrt(jnp.mean(o * o, axis=0, keepdims=True) + SUBLN_EPS) * sg_ref[...] * (1.0 - lambda_init)
    o_ref[...] = o.astype(BF16)


def _attention(qt, k, vt, lq1, lk1, lq2, lk2, sg, lambda_init, bsz, s_len, tq=256, tk=512):
    d = qt.shape[1]
    assert s_len % (2 * tk) == 0 and s_len % tq == 0, (s_len, tq, tk)
    k3 = k.reshape(bsz, s_len, d)
    cst = lambda a: pl.BlockSpec(a.shape, lambda b, h, i: (0, 0))
    return pl.pallas_call(
        functools.partial(_attn_body, lambda_init, tk),
        out_shape=jax.ShapeDtypeStruct((bsz, d, s_len), BF16),
        grid=(bsz, DIFF_HEADS, s_len // tq),
        in_specs=[pl.BlockSpec((None, LANES, tq), lambda b, h, i: (b, h, i)),
                  pl.BlockSpec((None, s_len, LANES), lambda b, h, i: (b, 0, h)),
                  pl.BlockSpec((None, LANES, s_len), lambda b, h, i: (b, h, 0)),
                  cst(lq1), cst(lk1), cst(lq2), cst(lk2), cst(sg)],
        out_specs=pl.BlockSpec((None, LANES, tq), lambda b, h, i: (b, h, i)),
        scratch_shapes=[pltpu.VMEM((2, tk, tq), F32), pltpu.VMEM((2, tk, tq), F32),
                        pltpu.VMEM((2, LANES + ATTN_ONES_ROWS, tq), F32)],
        compiler_params=_params("parallel", "parallel", "parallel"),
        name="diff_attn",
    )(qt, k3, vt, lq1, lk1, lq2, lk2, sg)


def _attn_out_body(x_ref, ot_ref, w_ref, g_ref, o_ref):
    h = _dot_tn(ot_ref[...], w_ref[...])
    o_ref[...] = x_ref[...] + _rms(h, g_ref[...], NORM_EPS)


def _attn_out(x2d, ot, w_o, g_post, bsz, s_len, tm=512):
    t, d = x2d.shape
    nb = s_len // tm
    cst = lambda a: pl.BlockSpec(a.shape, lambda b, i: (0, 0))
    return pl.pallas_call(
        _attn_out_body,
        out_shape=jax.ShapeDtypeStruct((t, d), F32),
        grid=(bsz, nb),
        in_specs=[pl.BlockSpec((tm, d), lambda b, i: (b * nb + i, 0)),
                  pl.BlockSpec((None, d, tm), lambda b, i: (b, 0, i)), cst(w_o), cst(g_post)],
        out_specs=pl.BlockSpec((tm, d), lambda b, i: (b * nb + i, 0)),
        compiler_params=_params("parallel", "parallel"),
        name="attn_out",
    )(x2d, ot, w_o, g_post)


def _block_diag2(a):
    z = jnp.zeros_like(a[0])
    return jnp.concatenate([jnp.concatenate([a[0], z], axis=1), jnp.concatenate([z, a[1]], axis=1)], axis=0)


def _rope_perm():
    half = HEAD_DIM // 2
    idx = []
    for h in range(DIFF_HEADS):
        for part in range(2):
            for c in range(2):
                idx.extend(h * 2 * HEAD_DIM + c * HEAD_DIM + part * half + np.arange(half))
    return np.asarray(idx)


def _rope_tables(s_len):
    half = HEAD_DIM // 2
    inv_freq = ROPE_THETA ** (-jnp.arange(half, dtype=F32) / half)
    ang = jnp.arange(s_len, dtype=F32)[:, None] * inv_freq[None, :]
    cos, sin = jnp.cos(ang), jnp.sin(ang)
    ck = jnp.concatenate([cos] * 4, axis=1)
    sk = jnp.concatenate([-sin, -sin, sin, sin], axis=1)
    return ck, sk, ck.T, sk.T


def kernel(x_prompt, x_sample, mix0_norm_pre, mix0_norm_post, w_in0, mu_prev, mu_next, decay_w0, decay_w2, iclr_a0, iclr_a2, gate_g2, k_k, k_a, r_k, lnx_g, lnx_b, w_out0, mix1_norm_pre, mix1_norm_post, w_qkv1, lambda_q1, lambda_k1, lambda_q2, lambda_k2, subln_g, w_o1, ffn_norm_pre, ffn_norm_post, w_up, conv_w, conv_b, w_down):
    w = RWKV_WIDTH
    ones_bd = jnp.asarray(np.kron(np.eye(w // HEAD_DIM), np.ones((HEAD_DIM, HEAD_DIM))), F32).astype(BF16)
    row = lambda a: a.reshape(1, -1)

    w_in_bf = w_in0[0].astype(BF16)
    tri_bd = np.kron(np.eye(RWKV_PREP_ROWS // CHUNK, dtype=np.float32), np.tril(np.ones((CHUNK, CHUNK), np.float32)))
    prep_consts = (row(mu_prev[0]), row(mu_next[0]),
                   row(decay_w0[0]), _block_diag2(decay_w2[0]).astype(BF16),
                   row(iclr_a0[0]), _block_diag2(iclr_a2[0]).astype(BF16),
                   gate_g2[0].astype(BF16), row(k_k[0]), row(k_a[0]), row(r_k[0]), ones_bd,
                   jnp.asarray(tri_bd).astype(BF16), jnp.asarray(tri_bd.T).astype(BF16))
    w_out_bf = w_out0[0].astype(BF16)

    perm = _rope_perm()
    dq = DIFF_HEADS * 2 * HEAD_DIM
    wq = w_qkv1[0][:, :dq][:, perm]
    wk = w_qkv1[0][:, dq:2 * dq][:, perm]
    wv = w_qkv1[0][:, 2 * dq:]
    wk_bf, wqt_bf, wvt_bf = wk.astype(BF16), wq.T.astype(BF16), wv.T.astype(BF16)
    w_o_bf = w_o1[0].astype(BF16)
    lambda_init = 0.8 - 0.6 * math.exp(-0.3 * 1)
    sg_col = subln_g[0].reshape(-1, 1)

    w_up_bf = w_up.astype(BF16)
    w_down_bf = w_down.astype(BF16)

    def ffn(x2d, bsz, s_len, layer):
        return _ffn(x2d, bsz, s_len, row(ffn_norm_pre[layer]), w_up_bf[layer], conv_w[layer],
                    row(conv_b[layer]), w_down_bf[layer], row(ffn_norm_post[layer]))

    def trunk(x):
        bsz, s_len, d = x.shape
        x2d = x.reshape(bsz * s_len, d)
        u, z = _in_proj(x2d, mix0_norm_pre[0], w_in_bf)
        f = _fnet(u, bsz, s_len, _dft_tables(s_len))
        r, v, kn, cumf, cumb, kf, kb, bf, bb, gate, bonus = _rwkv_prep(z, bsz, s_len, prep_consts)
        yf, yb = _wkv_scan(r, v, kn, cumf, cumb, kf, kb, bf, bb, bsz, s_len)
        x2d = _mix0_out(x2d, f, yf, yb, bonus, gate, row(lnx_g[0]), row(lnx_b[0]), ones_bd, w_out_bf,
                        row(mix0_norm_post[0]))
        x2d = ffn(x2d, bsz, s_len, 0)
        k, qt, vt = _qkv(x2d, bsz, s_len, row(mix1_norm_pre[0]), wk_bf, wqt_bf, wvt_bf, _rope_tables(s_len))
        ot = _attention(qt, k, vt, row(lambda_q1[0]), row(lambda_k1[0]), row(lambda_q2[0]), row(lambda_k2[0]),
                        sg_col, lambda_init, bsz, s_len)
        x2d = _attn_out(x2d, ot, w_o_bf, row(mix1_norm_post[0]), bsz, s_len)
        x2d = ffn(x2d, bsz, s_len, 1)
        return x2d.reshape(bsz, s_len, d)

    return (trunk(x_prompt), trunk(x_sample))
```

```python
import functools
import math

import numpy as np
import jax
import jax.numpy as jnp
from jax import lax
from jax.experimental import pallas as pl
from jax.experimental.pallas import tpu as pltpu

F32 = jnp.float32
BF16 = jnp.bfloat16

D_MODEL = 1024
HEAD_DIM = 64
FNET_WIDTH = 512
RWKV_WIDTH = 512
RWKV_HEADS = 8
RWKV_IN = 1920
D_FF = 2816
NORM_EPS = 1e-6
GN_EPS = 64e-5
SUBLN_EPS = 1e-5
ROPE_THETA = 10000.0
DIFF_HEADS = 8
LANES = 128
HALO = 8
CHUNK = 64
WKV_CHUNKS_PER_STEP = 2
RWKV_PREP_ROWS = 256
VMEM_LIMIT = 56 * 1024 * 1024
HIGHEST = lax.Precision.HIGHEST


def _params(*sem):
    return pltpu.CompilerParams(dimension_semantics=sem, vmem_limit_bytes=VMEM_LIMIT)


def _rms(x, g, eps):
    return x * lax.rsqrt(jnp.mean(x * x, axis=-1, keepdims=True) + eps) * g


def _dot(a, b):
    return jnp.dot(a, b, preferred_element_type=F32)


def _dot_nt(a, b):
    return lax.dot_general(a, b, (((1,), (1,)), ((), ())), preferred_element_type=F32)


def _dot_tn(a, b):
    return lax.dot_general(a, b, (((0,), (0,)), ((), ())), preferred_element_type=F32)


def _sigmoid(x):
    return 1.0 / (1.0 + jnp.exp(-x))


def _dot_split3(m_exact, x):
    h1 = x.astype(BF16)
    r1 = x - h1.astype(F32)
    h2 = r1.astype(BF16)
    h3 = (r1 - h2.astype(F32)).astype(BF16)
    return _dot(m_exact, h1) + _dot(m_exact, h2) + _dot(m_exact, h3)


def _segsum(x, ones_bd):
    hi = x.astype(BF16)
    lo = (x - hi.astype(F32)).astype(BF16)
    return _dot(hi, ones_bd) + _dot(lo, ones_bd)


def _in_proj_body(x_ref, g_ref, w_ref, u_ref, z_ref):
    h = _rms(x_ref[...], g_ref[...], NORM_EPS).astype(BF16)
    y = _dot(h, w_ref[...])
    u_ref[...] = y[:, :FNET_WIDTH].astype(BF16)
    z_ref[...] = y[:, FNET_WIDTH:]


def _in_proj(x2d, g, w, tm=512):
    t, d = x2d.shape
    n = w.shape[1]
    return pl.pallas_call(
        _in_proj_body,
        out_shape=(jax.ShapeDtypeStruct((t, FNET_WIDTH), BF16),
                   jax.ShapeDtypeStruct((t, n - FNET_WIDTH), F32)),
        grid=(t // tm,),
        in_specs=[pl.BlockSpec((tm, d), lambda i: (i, 0)),
                  pl.BlockSpec((1, d), lambda i: (0, 0)),
                  pl.BlockSpec((d, n), lambda i: (0, 0))],
        out_specs=(pl.BlockSpec((tm, FNET_WIDTH), lambda i: (i, 0)),
                   pl.BlockSpec((tm, n - FNET_WIDTH), lambda i: (i, 0))),
        compiler_params=_params("parallel"),
        name="in_proj",
    )(x2d, g.reshape(1, d), w)


def _rwkv_prep_body(z_ref, zp_ref, zn_ref, mup_ref, mun_ref, w0_ref, w2_ref, a0_ref, a2_ref, g2_ref,
                    kk_ref, ka_ref, rk_ref, ones_ref, trif_ref, trib_ref,
                    r_o, v_o, kn_o, lwf_o, lwb_o, kf_o, kb_o, bf_o, bb_o, gate_o, bonus_o):
    i = pl.program_id(1)
    n = pl.num_programs(1)
    z = z_ref[...]
    tm = z.shape[0]
    rows = lax.broadcasted_iota(jnp.int32, z.shape, 0)
    prev_row = jnp.where(i == 0, 0.0, zp_ref[HALO - 1:HALO, :])
    next_row = jnp.where(i == n - 1, 0.0, zn_ref[0:1, :])
    z_prev = jnp.where(rows == 0, prev_row, pltpu.roll(z, 1, 0))
    z_next = jnp.where(rows == tm - 1, next_row, pltpu.roll(z, tm - 1, 0))
    zs = z + mup_ref[...] * (z_prev - z) + mun_ref[...] * (z_next - z)

    w = RWKV_WIDTH
    r = zs[:, 0:w]
    k = zs[:, w:2 * w]
    v = zs[:, 2 * w:3 * w]
    wd = zs[:, 3 * w:3 * w + LANES]
    ad = zs[:, 3 * w + LANES:3 * w + 2 * LANES]
    gd = zs[:, 3 * w + 2 * LANES:3 * w + 3 * LANES]

    yw = w0_ref[...] + _dot(jnp.tanh(wd).astype(BF16), w2_ref[...])
    lw = -math.exp(-0.5) * _sigmoid(yw)
    iclr = _sigmoid(a0_ref[...] + _dot(ad.astype(BF16), a2_ref[...]))
    gate = _dot(_sigmoid(gd).astype(BF16), g2_ref[...])

    ones_bd = ones_ref[...]
    kk = k * kk_ref[...]
    kn = kk / jnp.maximum(jnp.sqrt(_segsum(kk * kk, ones_bd)), 1e-12)
    ka = ka_ref[...]
    k_f = k * (1.0 + (iclr[:, :w] - 1.0) * ka)
    k_b = k * (1.0 + (iclr[:, w:] - 1.0) * ka)
    bonus = _segsum(r * rk_ref[...] * (k_f + k_b), ones_bd)

    r_o[...] = r
    v_o[...] = v
    kn_o[...] = kn
    lwf_o[...] = _dot_split3(trif_ref[...], lw[:, :w])
    lwb_o[...] = _dot_split3(trib_ref[...], lw[:, w:])
    kf_o[...] = k_f
    kb_o[...] = k_b
    bf_o[...] = kn * iclr[:, :w]
    bb_o[...] = kn * iclr[:, w:]
    gate_o[...] = gate
    bonus_o[...] = bonus * v


def _rwkv_prep(z, bsz, s_len, consts, tm=RWKV_PREP_ROWS):
    t, n = z.shape
    nb = s_len // tm
    nh = t // HALO
    w = RWKV_WIDTH

    def full(a):
        return pl.BlockSpec(a.shape, lambda b, i: (0,) * a.ndim)

    out = jax.ShapeDtypeStruct((t, w), F32)
    return pl.pallas_call(
        _rwkv_prep_body,
        out_shape=(out,) * 11,
        grid=(bsz, nb),
        in_specs=[pl.BlockSpec((tm, n), lambda b, i: (b * nb + i, 0)),
                  pl.BlockSpec((HALO, n), lambda b, i: (jnp.maximum((b * nb + i) * (tm // HALO) - 1, 0), 0)),
                  pl.BlockSpec((HALO, n), lambda b, i: (jnp.minimum((b * nb + i + 1) * (tm // HALO), nh - 1), 0)),
                  ] + [full(a) for a in consts],
        out_specs=(pl.BlockSpec((tm, w), lambda b, i: (b * nb + i, 0)),) * 11,
        compiler_params=_params("parallel", "parallel"),
        name="rwkv_prep",
    )(z, z, z, *consts)


def _wkv_items(reverse, r_ref, v_ref, kn_ref, cum_ref, k_ref, b_ref, y_ref, ht_ref):
    c = CHUNK
    half = LANES // 2
    row2 = lax.broadcasted_iota(jnp.int32, (2 * c, LANES), 0)
    lane2 = lax.broadcasted_iota(jnp.int32, (2 * c, LANES), 1)
    t_row, t_col = row2 % c, lane2 % c
    own = (row2 // c) == (lane2 // HEAD_DIM)
    first = row2 < c
    order = (t_row < t_col, t_row <= t_col) if reverse else (t_row > t_col, t_row >= t_col)
    strict, incl = own & order[0], own & order[1]
    lane = lax.broadcasted_iota(jnp.int32, (c, LANES), 1)
    m0 = lane < HEAD_DIM

    def stack(x):
        return jnp.concatenate([jnp.where(m0, x, 0.0), jnp.where(m0, 0.0, x)], axis=0)

    trow = lax.broadcasted_iota(jnp.int32, (c, LANES), 0)
    items = []
    subs = range(WKV_CHUNKS_PER_STEP)
    for sub, p in [(s, p) for s in (reversed(subs) if reverse else subs) for p in range(RWKV_WIDTH // LANES)]:
        rows = slice(sub * c, (sub + 1) * c)
        sl = slice(p * LANES, (p + 1) * LANES)
        cum = cum_ref[rows, sl]
        if reverse:
            cum_ex = jnp.where(trow == c - 1, 0.0, pltpu.roll(cum, c - 1, 0))
            tot = cum[0:1]
        else:
            cum_ex = jnp.where(trow == 0, 0.0, pltpu.roll(cum, 1, 0))
            tot = cum[c - 1:c]
        r, v, kn, k, b = r_ref[rows, sl], v_ref[rows, sl], kn_ref[rows, sl], k_ref[rows, sl], b_ref[rows, sl]
        e_neg = jnp.exp(-cum)
        e_end = jnp.exp(tot - cum)
        am2 = stack(-kn * jnp.exp(cum_ex))
        rm2 = stack(r * jnp.exp(cum))
        vm2 = stack(v).astype(BF16)
        rhs = jnp.concatenate([b * e_neg, k * e_neg], axis=0).astype(BF16)
        aa = _dot_nt(jnp.concatenate([am2, rm2], axis=0).astype(BF16), rhs)
        ar = pltpu.roll(aa, half, 1)
        aa_a, ar_a, aa_r, ar_r = aa[:2 * c], ar[:2 * c], aa[2 * c:], ar[2 * c:]
        items.append(dict(
            p=p, rows=rows, sl=sl, y_ref=y_ref, ht_ref=ht_ref, own=own, v=v, am2=am2, rm2=rm2, vm2=vm2,
            etot=jnp.exp(tot),
            bk=jnp.concatenate([b * e_end, k * e_end], axis=0).astype(BF16),
            abd_b=jnp.where(strict, jnp.where(first, aa_a, ar_a), 0.0),
            abd_k=jnp.where(strict, jnp.where(first, ar_a, aa_a), 0.0).astype(BF16),
            rbd=jnp.concatenate([jnp.where(incl, jnp.where(first, aa_r, ar_r), 0.0),
                                 jnp.where(incl, jnp.where(first, ar_r, aa_r), 0.0)], axis=1).astype(BF16)))
    return items


def _wkv_solve(items):
    half = LANES // 2
    for it in items:
        it["x"] = it["am2"] + pltpu.roll(_dot(it["abd_k"], it["vm2"]), half, 1)
        it["pw"] = it["abd_b"]
    n_lvl = int(math.log2(CHUNK))
    for lvl in range(n_lvl):
        for it in items:
            x, pw = it["x"], it["pw"].astype(BF16)
            if lvl < n_lvl - 1:
                pr = _dot(pw, jnp.concatenate([x, it["pw"]], axis=1).astype(BF16))
                it["x"] = x + pr[:, :LANES]
                it["pw"] = pr[:, LANES:]
            else:
                it["x"] = x + _dot(pw, x.astype(BF16))


def _wkv_finish(items):
    c = CHUNK
    half = LANES // 2
    r128 = lax.broadcasted_iota(jnp.int32, (LANES, LANES), 0)
    c128 = lax.broadcasted_iota(jnp.int32, (LANES, LANES), 1)
    same_head = (r128 // HEAD_DIM) == (c128 // HEAD_DIM)
    for it in items:
        x, own, rbd, bk = it["x"], it["own"], it["rbd"], it["bk"]
        p1_2 = jnp.where(own, x, 0.0)
        p2_2 = jnp.where(own, pltpu.roll(x, half, 1), 0.0)
        q1_2 = it["rm2"] + _dot(rbd[:, :LANES], p1_2.astype(BF16))
        q2_2 = _dot(rbd, jnp.concatenate([p2_2.astype(BF16), it["vm2"]], axis=0))
        p1, p2 = p1_2[:c] + p1_2[c:], p2_2[:c] + p2_2[c:]
        q1, q2 = q1_2[:c] + q1_2[c:], q2_2[:c] + q2_2[c:]
        bp = jnp.where(same_head, _dot_tn(bk[:c], p1.astype(BF16)), 0.0)
        gt = jnp.where(same_head, _dot_tn(jnp.concatenate([p2, it["v"]], axis=0).astype(BF16), bk), 0.0)
        ht = it["ht_ref"][it["p"]]
        hb = ht.astype(BF16)
        it["y_ref"][it["rows"], it["sl"]] = _dot_nt(q1.astype(BF16), hb) + q2
        it["ht_ref"][it["p"]] = it["etot"] * ht + _dot_nt(hb, bp.astype(BF16)) + gt


def _wkv_body(rf, vf, knf, cumf, kf, bf, rb, vb, knb, cumb, kb, bb, yf, yb, hf_ref, hb_ref):
    @pl.when(pl.program_id(1) == 0)
    def _():
        hf_ref[...] = jnp.zeros_like(hf_ref)
        hb_ref[...] = jnp.zeros_like(hb_ref)

    items = (_wkv_items(False, rf, vf, knf, cumf, kf, bf, yf, hf_ref)
             + _wkv_items(True, rb, vb, knb, cumb, kb, bb, yb, hb_ref))
    _wkv_solve(items)
    _wkv_finish(items)


def _wkv_scan(r, v, kn, cumf, cumb, kf, kb, bf, bb, bsz, s_len):
    t, w = r.shape
    c = CHUNK * WKV_CHUNKS_PER_STEP
    nc = s_len // c
    fwd = pl.BlockSpec((c, w), lambda b, i: (b * nc + i, 0))
    bwd = pl.BlockSpec((c, w), lambda b, i: (b * nc + nc - 1 - i, 0))
    out = jax.ShapeDtypeStruct((t, w), F32)
    n_pairs = w // LANES
    return pl.pallas_call(
        _wkv_body,
        out_shape=(out, out),
        grid=(bsz, nc),
        in_specs=[fwd] * 6 + [bwd] * 6,
        out_specs=(fwd, bwd),
        scratch_shapes=[pltpu.VMEM((n_pairs, LANES, LANES), F32),
                        pltpu.VMEM((n_pairs, LANES, LANES), F32)],
        compiler_params=_params("parallel", "arbitrary"),
        name="wkv_scan",
    )(r, v, kn, cumf, kf, bf, r, v, kn, cumb, kb, bb)


def _dft_split(s_len):
    s1 = 1 << ((s_len.bit_length() - 1 + 1) // 2)
    return s1, s_len // s1


def _dft_tables(s_len):
    s1, s2 = _dft_split(s_len)

    def cs(num, period):
        ang = (num % period).astype(F32) * (2.0 * math.pi / period)
        return jnp.cos(ang), jnp.sin(ang)

    k1 = jnp.arange(s1, dtype=jnp.int32)
    c1, sn1 = cs(k1[:, None] * k1[None, :], s1)
    w1 = jnp.concatenate([c1, sn1], axis=0)
    k = k1[:, None, None] + s1 * jnp.arange(s2, dtype=jnp.int32)[None, :, None]
    ec, es = cs(k * jnp.arange(s2, dtype=jnp.int32)[None, None, :], s_len)
    e = jnp.concatenate([jnp.concatenate([ec, -es], axis=2),
                         jnp.concatenate([-es, -ec], axis=2)], axis=1)
    c = jnp.arange(HEAD_DIM, dtype=jnp.int32)
    c3, s3 = cs(c[:, None] * c[None, :], HEAD_DIM)
    eye = jnp.eye(2 * LANES // HEAD_DIM, dtype=F32)
    return (w1.astype(BF16), e.astype(BF16), jnp.kron(eye, c3).astype(BF16), jnp.kron(eye, s3).astype(BF16))


def _dft1_body(w_ref, u_ref, y_ref):
    y_ref[...] = _dot(w_ref[...], u_ref[...]).astype(BF16)


def _dft2_body(scale, e_ref, y_ref, c4_ref, s4_ref, f_ref):
    g = e_ref.shape[0]
    s2 = y_ref.shape[2]
    nw = 2 * LANES
    for j in range(g):
        yy = jnp.concatenate([y_ref[0, j], y_ref[1, j]], axis=0)
        x = _dot(e_ref[j], yy)
        xr = x[:s2].astype(BF16)
        xi = x[s2:].astype(BF16)
        for q in range(FNET_WIDTH // nw):
            f = _dot(xr[:, q * nw:(q + 1) * nw], c4_ref[...]) + _dot(xi[:, q * nw:(q + 1) * nw], s4_ref[...])
            f_ref[:, j * FNET_WIDTH + q * nw:j * FNET_WIDTH + (q + 1) * nw] = f * scale


def _fnet(u, bsz, s_len, tables):
    w1, e, c4, s4 = tables
    s1, s2 = _dft_split(s_len)
    fw = FNET_WIDTH
    ncol = s2 * fw
    nb = min(ncol, 4096)
    u3 = u.reshape(bsz, s1, ncol)
    y = pl.pallas_call(
        _dft1_body,
        out_shape=jax.ShapeDtypeStruct((bsz, 2 * s1, ncol), BF16),
        grid=(bsz, ncol // nb),
        in_specs=[pl.BlockSpec((2 * s1, s1), lambda b, j: (0, 0)),
                  pl.BlockSpec((None, s1, nb), lambda b, j: (b, 0, j))],
        out_specs=pl.BlockSpec((None, 2 * s1, nb), lambda b, j: (b, 0, j)),
        compiler_params=_params("parallel", "parallel"),
        name="dft_stage1",
    )(w1, u3)
    y5 = y.reshape(bsz, 2, s1, s2, fw)
    g = 8
    scale = 1.0 / math.sqrt(s_len * HEAD_DIM)
    f = pl.pallas_call(
        functools.partial(_dft2_body, scale),
        out_shape=jax.ShapeDtypeStruct((bsz, s2, s1 * fw), F32),
        grid=(bsz, s1 // g),
        in_specs=[pl.BlockSpec((g, 2 * s2, 2 * s2), lambda b, j: (j, 0, 0)),
                  pl.BlockSpec((None, 2, g, s2, fw), lambda b, j: (b, 0, j, 0, 0)),
                  pl.BlockSpec(c4.shape, lambda b, j: (0, 0)),
                  pl.BlockSpec(s4.shape, lambda b, j: (0, 0))],
        out_specs=pl.BlockSpec((None, s2, g * fw), lambda b, j: (b, 0, j)),
        compiler_params=_params("parallel", "parallel"),
        name="dft_stage2",
    )(e, y5, c4, s4)
    return f.reshape(bsz * s_len, fw)


def _mix0_out_body(x_ref, f_ref, yf_ref, yb_ref, bonus_ref, gate_ref, lng_ref, lnb_ref, ones_ref, w_ref, g_ref,
                   o_ref):
    ones_bd = ones_ref[...]
    y = yf_ref[...] + yb_ref[...]
    mu = _segsum(y, ones_bd) * (1.0 / HEAD_DIM)
    d = y - mu
    var = _segsum(d * d, ones_bd) * (1.0 / HEAD_DIM)
    yn = d * lax.rsqrt(var + GN_EPS) * lng_ref[...] + lnb_ref[...]
    o = (yn + bonus_ref[...]) * gate_ref[...]
    cat = jnp.concatenate([f_ref[...].astype(BF16), o.astype(BF16)], axis=1)
    h = _dot(cat, w_ref[...])
    o_ref[...] = x_ref[...] + _rms(h, g_ref[...], NORM_EPS)


def _mix0_out(x2d, f, yf, yb, bonus, gate, lng, lnb, ones_bd, w_out, g_post, tm=512):
    t, d = x2d.shape
    w = RWKV_WIDTH
    row = lambda n: pl.BlockSpec((tm, n), lambda i: (i, 0))
    cst = lambda a: pl.BlockSpec(a.shape, lambda i: (0, 0))
    return pl.pallas_call(
        _mix0_out_body,
        out_shape=jax.ShapeDtypeStruct((t, d), F32),
        grid=(t // tm,),
        in_specs=[row(d), row(w), row(w), row(w), row(w), row(w),
                  cst(lng), cst(lnb), cst(ones_bd), cst(w_out), cst(g_post)],
        out_specs=row(d),
        compiler_params=_params("parallel"),
        name="mix0_out",
    )(x2d, f, yf, yb, bonus, gate, lng, lnb, ones_bd, w_out, g_post)


FF_CHUNK = 256


def _ffn_body(x_ref, xp_ref, xn_ref, gpre_ref, wup_ref, cw_ref, cb_ref, wdn_ref, gpost_ref, o_ref,
              h_ref, acc_ref, ua_ref, ub_ref):
    i = pl.program_id(1)
    n = pl.num_programs(1)
    tm = x_ref.shape[0]
    f = FF_CHUNK
    g = gpre_ref[...]
    x = x_ref[...]
    hp = jnp.where(i == 0, 0.0, _rms(xp_ref[...], g, NORM_EPS))
    hn = jnp.where(i == n - 1, 0.0, _rms(xn_ref[...], g, NORM_EPS))
    h_ref[...] = jnp.concatenate([hp, _rms(x, g, NORM_EPS), hn], axis=0).astype(BF16)
    acc_ref[...] = jnp.zeros_like(acc_ref)

    def cols(j):
        return j * f, D_FF + j * f

    def up(j, dst):
        h = h_ref[...]
        for part, c0 in enumerate(cols(j)):
            dst[:, part * f:(part + 1) * f] = _dot(h, wup_ref[:, pl.ds(c0, f)])

    def post(j, src):
        def conv(part, c0):
            u = src[:, part * f:(part + 1) * f]
            cw = cw_ref[:, pl.ds(c0, f)]
            return (cw[0:1] * u[HALO - 1:HALO - 1 + tm] + cw[1:2] * u[HALO:HALO + tm]
                    + cw[2:3] * u[HALO + 1:HALO + 1 + tm] + cb_ref[:, pl.ds(c0, f)])

        cv, cg = cols(j)
        val = conv(0, cv)
        gate = conv(1, cg)
        act = 0.5 * gate * (1.0 + jnp.tanh(math.sqrt(2.0 / math.pi) * (gate + 0.044715 * gate * gate * gate)))
        acc_ref[...] += _dot((act * val).astype(BF16), wdn_ref[pl.ds(cv, f), :])

    n_chunks = D_FF // f
    bufs = (ua_ref, ub_ref)
    up(0, ua_ref)
    for j in range(n_chunks):
        if j + 1 < n_chunks:
            up(j + 1, bufs[(j + 1) % 2])
        post(j, bufs[j % 2])
    o_ref[...] = x + _rms(acc_ref[...], gpost_ref[...], NORM_EPS)


def _ffn(x2d, bsz, s_len, g_pre, w_up, conv_w, conv_b, w_down, g_post, tm=512):
    t, d = x2d.shape
    nb = s_len // tm
    nh = t // HALO
    cst = lambda a: pl.BlockSpec(a.shape, lambda b, i: (0,) * a.ndim, pipeline_mode=pl.Buffered(1))
    return pl.pallas_call(
        _ffn_body,
        out_shape=jax.ShapeDtypeStruct((t, d), F32),
        grid=(bsz, nb),
        in_specs=[pl.BlockSpec((tm, d), lambda b, i: (b * nb + i, 0)),
                  pl.BlockSpec((HALO, d), lambda b, i: (jnp.maximum((b * nb + i) * (tm // HALO) - 1, 0), 0)),
                  pl.BlockSpec((HALO, d), lambda b, i: (jnp.minimum((b * nb + i + 1) * (tm // HALO), nh - 1), 0)),
                  cst(g_pre), cst(w_up), cst(conv_w), cst(conv_b), cst(w_down), cst(g_post)],
        out_specs=pl.BlockSpec((tm, d), lambda b, i: (b * nb + i, 0)),
        scratch_shapes=[pltpu.VMEM((tm + 2 * HALO, d), BF16), pltpu.VMEM((tm, d), F32),
                        pltpu.VMEM((tm + 2 * HALO, 2 * FF_CHUNK), F32),
                        pltpu.VMEM((tm + 2 * HALO, 2 * FF_CHUNK), F32)],
        compiler_params=_params("parallel", "parallel"),
        name="conv_ffn",
    )(x2d, x2d, x2d, g_pre, w_up, conv_w, conv_b, w_down, g_post)


def _qkv_body(x_ref, g_ref, wk_ref, wqt_ref, wvt_ref, ck_ref, sk_ref, cq_ref, sq_ref, k_o, qt_o, vt_o):
    h = _rms(x_ref[...], g_ref[...], NORM_EPS).astype(BF16)
    ck, sk = ck_ref[...], sk_ref[...]
    cq, sq = cq_ref[...], sq_ref[...]
    half = LANES // 2
    for hp in range(DIFF_HEADS // 2):
        kp = _dot(h, wk_ref[:, hp * 2 * LANES:(hp + 1) * 2 * LANES])
        for e in range(2):
            kh = kp[:, e * LANES:(e + 1) * LANES]
            sl = slice((2 * hp + e) * LANES, (2 * hp + e + 1) * LANES)
            k_o[:, sl] = (kh * ck + pltpu.roll(kh, half, 1) * sk).astype(BF16)
    qt_all = _dot_nt(wqt_ref[...], h)
    vt_o[...] = _dot_nt(wvt_ref[...], h).astype(BF16)
    for hd in range(DIFF_HEADS):
        sl = slice(hd * LANES, (hd + 1) * LANES)
        qh = qt_all[sl]
        qsw = jnp.concatenate([qh[half:], qh[:half]], axis=0)
        qt_o[sl, :] = ((qh * cq + qsw * sq) * (HEAD_DIM ** -0.5 * math.log2(math.e))).astype(BF16)


def _qkv(x2d, bsz, s_len, g, wk, wqt, wvt, tabs, tm=512):
    t, d = x2d.shape
    nb = s_len // tm
    ck, sk, cq, sq = tabs
    cst = lambda a: pl.BlockSpec(a.shape, lambda b, i: (0, 0))
    tshape = jax.ShapeDtypeStruct((bsz, d, s_len), BF16)
    return pl.pallas_call(
        _qkv_body,
        out_shape=(jax.ShapeDtypeStruct((t, d), BF16), tshape, tshape),
        grid=(bsz, nb),
        in_specs=[pl.BlockSpec((tm, d), lambda b, i: (b * nb + i, 0)), cst(g), cst(wk), cst(wqt), cst(wvt),
                  pl.BlockSpec((tm, LANES), lambda b, i: (i, 0)),
                  pl.BlockSpec((tm, LANES), lambda b, i: (i, 0)),
                  pl.BlockSpec((LANES, tm), lambda b, i: (0, i)),
                  pl.BlockSpec((LANES, tm), lambda b, i: (0, i))],
        out_specs=(pl.BlockSpec((tm, d), lambda b, i: (b * nb + i, 0)),
                   pl.BlockSpec((None, d, tm), lambda b, i: (b, 0, i)),
                   pl.BlockSpec((None, d, tm), lambda b, i: (b, 0, i))),
        compiler_params=_params("parallel", "parallel"),
        name="qkv_rope",
    )(x2d, g, wk, wqt, wvt, ck, sk, cq, sq)


NEG_BIG = -1e30
ATTN_ONES_ROWS = 16


def _attn_split_q(qt):
    rowi = lax.broadcasted_iota(jnp.int32, qt.shape, 0)
    first = (rowi // (HEAD_DIM // 2)) % 2 == 0
    zero = jnp.zeros_like(qt)
    return jnp.where(first, qt, zero), jnp.where(first, zero, qt)


def _attn_finish(lambda_init, acc0, acc1, lq1, lk1, lq2, lk2, sg_ref, o_ref):
    lam = (jnp.exp(jnp.sum(lq1[...] * lk1[...], axis=1, keepdims=True))
           - jnp.exp(jnp.sum(lq2[...] * lk2[...], axis=1, keepdims=True)) + lambda_init)
    o = acc0[:LANES] / acc0[LANES:LANES + 1] - lam * (acc1[:LANES] / acc1[LANES:LANES + 1])
    o = o * lax.rsqrt(jnp.mean(o * o, axis=0, keepdims=True) + SUBLN_EPS) * sg_ref[...] * (1.0 - lambda_init)
    o_ref[...] = o.astype(BF16)


def _attn_body(lambda_init, tk, qt_ref, k_ref, vt_ref, lq1, lk1, lq2, lk2, sg_ref, o_ref, sa_ref, sb_ref, acc_ref):
    qh = _attn_split_q(qt_ref[...])
    tq = qt_ref.shape[1]
    s_len = k_ref.shape[0]

    def scores(j, dst):
        kk = k_ref[pl.ds(pl.multiple_of(j * tk, tk), tk), :]
        for c in range(2):
            dst[c] = _dot(kk, qh[c])

    ones_row = (lax.broadcasted_iota(jnp.int32, (ATTN_ONES_ROWS, tk), 0) == 0).astype(BF16)

    def update(j, src, ms):
        vt = vt_ref[:, pl.ds(pl.multiple_of(j * tk, tk), tk)]
        vte = jnp.concatenate([vt, ones_row], axis=0)
        out = []
        for c in range(2):
            s = src[c]
            mn = jnp.maximum(ms[c], jnp.max(s, axis=0, keepdims=True))
            p = jnp.exp2(s - mn).astype(BF16)
            acc_ref[c] = jnp.exp2(ms[c] - mn) * acc_ref[c] + _dot(vte, p)
            out.append(mn)
        return tuple(out)

    mi = jnp.full((1, tq), NEG_BIG, F32)
    acc_ref[...] = jnp.zeros_like(acc_ref)
    n_tiles = s_len // tk

    def step(i, ms):
        scores(2 * i + 1, sb_ref)
        ms = update(2 * i, sa_ref, ms)
        scores(2 * i + 2, sa_ref)
        return update(2 * i + 1, sb_ref, ms)

    scores(0, sa_ref)
    trips = n_tiles // 2 - 1
    ms = lax.fori_loop(0, trips, step, (mi, mi), unroll=2 if trips >= 4 else 1)
    scores(n_tiles - 1, sb_ref)
    ms = update(n_tiles - 2, sa_ref, ms)
    update(n_tiles - 1, sb_ref, ms)
    _attn_finish(lambda_init, acc_ref[0], acc_ref[1], lq1, lk1, lq2, lk2, sg_ref, o_ref)


def _attention(qt, k, vt, lq1, lk1, lq2, lk2, sg, lambda_init, bsz, s_len, tq=256, tk=512):
    d = qt.shape[1]
    assert s_len % (2 * tk) == 0 and s_len % tq == 0, (s_len, tq, tk)
    k3 = k.reshape(bsz, s_len, d)
    cst = lambda a: pl.BlockSpec(a.shape, lambda b, h, i: (0, 0))
    scratch = [pltpu.VMEM((2, tk, tq), F32), pltpu.VMEM((2, tk, tq), F32),
               pltpu.VMEM((2, LANES + ATTN_ONES_ROWS, tq), F32)]
    return pl.pallas_call(
        functools.partial(_attn_body, lambda_init, tk),
        out_shape=jax.ShapeDtypeStruct((bsz, d, s_len), BF16),
        grid=(bsz, DIFF_HEADS, s_len // tq),
        in_specs=[pl.BlockSpec((None, LANES, tq), lambda b, h, i: (b, h, i)),
                  pl.BlockSpec((None, s_len, LANES), lambda b, h, i: (b, 0, h)),
                  pl.BlockSpec((None, LANES, s_len), lambda b, h, i: (b, h, 0)),
                  cst(lq1), cst(lk1), cst(lq2), cst(lk2), cst(sg)],
        out_specs=pl.BlockSpec((None, LANES, tq), lambda b, h, i: (b, h, i)),
        scratch_shapes=scratch,
        compiler_params=_params("parallel", "parallel", "parallel"),
        name="diff_attn",
    )(qt, k3, vt, lq1, lk1, lq2, lk2, sg)


def _attn_out_body(x_ref, ot_ref, w_ref, g_ref, o_ref):
    h = _dot_tn(ot_ref[...], w_ref[...])
    o_ref[...] = x_ref[...] + _rms(h, g_ref[...], NORM_EPS)


def _attn_out(x2d, ot, w_o, g_post, bsz, s_len, tm=512):
    t, d = x2d.shape
    nb = s_len // tm
    cst = lambda a: pl.BlockSpec(a.shape, lambda b, i: (0, 0))
    return pl.pallas_call(
        _attn_out_body,
        out_shape=jax.ShapeDtypeStruct((t, d), F32),
        grid=(bsz, nb),
        in_specs=[pl.BlockSpec((tm, d), lambda b, i: (b * nb + i, 0)),
                  pl.BlockSpec((None, d, tm), lambda b, i: (b, 0, i)), cst(w_o), cst(g_post)],
        out_specs=pl.BlockSpec((tm, d), lambda b, i: (b * nb + i, 0)),
        compiler_params=_params("parallel", "parallel"),
        name="attn_out",
    )(x2d, ot, w_o, g_post)


def _block_diag2(a):
    z = jnp.zeros_like(a[0])
    return jnp.concatenate([jnp.concatenate([a[0], z], axis=1), jnp.concatenate([z, a[1]], axis=1)], axis=0)


def _rope_perm():
    half = HEAD_DIM // 2
    idx = []
    for h in range(DIFF_HEADS):
        for part in range(2):
            for c in range(2):
                idx.extend(h * 2 * HEAD_DIM + c * HEAD_DIM + part * half + np.arange(half))
    return np.asarray(idx)


def _rope_tables(s_len):
    half = HEAD_DIM // 2
    inv_freq = ROPE_THETA ** (-jnp.arange(half, dtype=F32) / half)
    ang = jnp.arange(s_len, dtype=F32)[:, None] * inv_freq[None, :]
    cos, sin = jnp.cos(ang), jnp.sin(ang)
    ck = jnp.concatenate([cos] * 4, axis=1)
    sk = jnp.concatenate([-sin, -sin, sin, sin], axis=1)
    return ck, sk, ck.T, sk.T


def kernel(x_prompt, x_sample, mix0_norm_pre, mix0_norm_post, w_in0, mu_prev, mu_next, decay_w0, decay_w2, iclr_a0, iclr_a2, gate_g2, k_k, k_a, r_k, lnx_g, lnx_b, w_out0, mix1_norm_pre, mix1_norm_post, w_qkv1, lambda_q1, lambda_k1, lambda_q2, lambda_k2, subln_g, w_o1, ffn_norm_pre, ffn_norm_post, w_up, conv_w, conv_b, w_down):
    w = RWKV_WIDTH
    ones_bd = jnp.asarray(np.kron(np.eye(w // HEAD_DIM), np.ones((HEAD_DIM, HEAD_DIM))), F32).astype(BF16)
    row = lambda a: a.reshape(1, -1)

    w_in_bf = w_in0[0].astype(BF16)
    tri_bd = np.kron(np.eye(RWKV_PREP_ROWS // CHUNK, dtype=np.float32), np.tril(np.ones((CHUNK, CHUNK), np.float32)))
    prep_consts = (row(mu_prev[0]), row(mu_next[0]),
                   row(decay_w0[0]), _block_diag2(decay_w2[0]).astype(BF16),
                   row(iclr_a0[0]), _block_diag2(iclr_a2[0]).astype(BF16),
                   gate_g2[0].astype(BF16), row(k_k[0]), row(k_a[0]), row(r_k[0]), ones_bd,
                   jnp.asarray(tri_bd).astype(BF16), jnp.asarray(tri_bd.T).astype(BF16))
    w_out_bf = w_out0[0].astype(BF16)

    perm = _rope_perm()
    dq = DIFF_HEADS * 2 * HEAD_DIM
    wq = w_qkv1[0][:, :dq][:, perm]
    wk = w_qkv1[0][:, dq:2 * dq][:, perm]
    wv = w_qkv1[0][:, 2 * dq:]
    wk_bf, wqt_bf, wvt_bf = wk.astype(BF16), wq.T.astype(BF16), wv.T.astype(BF16)
    w_o_bf = w_o1[0].astype(BF16)
    lambda_init = 0.8 - 0.6 * math.exp(-0.3 * 1)
    sg_col = subln_g[0].reshape(-1, 1)

    w_up_bf = w_up.astype(BF16)
    w_down_bf = w_down.astype(BF16)

    def ffn(x2d, bsz, s_len, layer):
        return _ffn(x2d, bsz, s_len, row(ffn_norm_pre[layer]), w_up_bf[layer], conv_w[layer],
                    row(conv_b[layer]), w_down_bf[layer], row(ffn_norm_post[layer]))

    def trunk(x):
        bsz, s_len, d = x.shape
        x2d = x.reshape(bsz * s_len, d)
        u, z = _in_proj(x2d, mix0_norm_pre[0], w_in_bf)
        f = _fnet(u, bsz, s_len, _dft_tables(s_len))
        r, v, kn, cumf, cumb, kf, kb, bf, bb, gate, bonus = _rwkv_prep(z, bsz, s_len, prep_consts)
        yf, yb = _wkv_scan(r, v, kn, cumf, cumb, kf, kb, bf, bb, bsz, s_len)
        x2d = _mix0_out(x2d, f, yf, yb, bonus, gate, row(lnx_g[0]), row(lnx_b[0]), ones_bd, w_out_bf,
                        row(mix0_norm_post[0]))
        x2d = ffn(x2d, bsz, s_len, 0)
        k, qt, vt = _qkv(x2d, bsz, s_len, row(mix1_norm_pre[0]), wk_bf, wqt_bf, wvt_bf, _rope_tables(s_len))
        ot = _attention(qt, k, vt, row(lambda_q1[0]), row(lambda_k1[0]), row(lambda_q2[0]), row(lambda_k2[0]),
                        sg_col, lambda_init, bsz, s_len)
        x2d = _attn_out(x2d, ot, w_o_bf, row(mix1_norm_post[0]), bsz, s_len)
        x2d = ffn(x2d, bsz, s_len, 1)
        return x2d.reshape(bsz, s_len, d)

    return (trunk(x_prompt), trunk(x_sample))
```

```python
import functools
import math

import numpy as np
import jax
import jax.numpy as jnp
from jax import lax
from jax.experimental import pallas as pl
from jax.experimental.pallas import tpu as pltpu

F32 = jnp.float32
BF16 = jnp.bfloat16

HEAD_DIM = 64
FNET_WIDTH = 512
RWKV_WIDTH = 512
D_FF = 2816
NORM_EPS = 1e-6
GN_EPS = 64e-5
SUBLN_EPS = 1e-5
ROPE_THETA = 10000.0
DIFF_HEADS = 8
LANES = 128
HALO = 8
CHUNK = 64
WKV_CHUNKS_PER_STEP = 2
RWKV_PREP_ROWS = 256
VMEM_LIMIT = 56 * 1024 * 1024


def _params(*sem):
    return pltpu.CompilerParams(dimension_semantics=sem, vmem_limit_bytes=VMEM_LIMIT)


def _rms(x, g, eps):
    return x * lax.rsqrt(jnp.mean(x * x, axis=-1, keepdims=True) + eps) * g


def _dot(a, b):
    return jnp.dot(a, b, preferred_element_type=F32)


def _dot_nt(a, b):
    return lax.dot_general(a, b, (((1,), (1,)), ((), ())), preferred_element_type=F32)


def _dot_tn(a, b):
    return lax.dot_general(a, b, (((0,), (0,)), ((), ())), preferred_element_type=F32)


def _sigmoid(x):
    return 1.0 / (1.0 + jnp.exp(-x))


def _dot_split3(m_exact, x):
    h1 = x.astype(BF16)
    r1 = x - h1.astype(F32)
    h2 = r1.astype(BF16)
    h3 = (r1 - h2.astype(F32)).astype(BF16)
    return _dot(m_exact, h1) + _dot(m_exact, h2) + _dot(m_exact, h3)


def _segsum(x, ones_bd):
    hi = x.astype(BF16)
    lo = (x - hi.astype(F32)).astype(BF16)
    return _dot(hi, ones_bd) + _dot(lo, ones_bd)


def _in_proj_body(x_ref, g_ref, w_ref, u_ref, z_ref):
    h = _rms(x_ref[...], g_ref[...], NORM_EPS).astype(BF16)
    y = _dot(h, w_ref[...])
    u_ref[...] = y[:, :FNET_WIDTH].astype(BF16)
    z_ref[...] = y[:, FNET_WIDTH:]


def _in_proj(x2d, g, w, tm=512):
    t, d = x2d.shape
    n = w.shape[1]
    return pl.pallas_call(
        _in_proj_body,
        out_shape=(jax.ShapeDtypeStruct((t, FNET_WIDTH), BF16),
                   jax.ShapeDtypeStruct((t, n - FNET_WIDTH), F32)),
        grid=(t // tm,),
        in_specs=[pl.BlockSpec((tm, d), lambda i: (i, 0)),
                  pl.BlockSpec((1, d), lambda i: (0, 0)),
                  pl.BlockSpec((d, n), lambda i: (0, 0))],
        out_specs=(pl.BlockSpec((tm, FNET_WIDTH), lambda i: (i, 0)),
                   pl.BlockSpec((tm, n - FNET_WIDTH), lambda i: (i, 0))),
        compiler_params=_params("parallel"),
        name="in_proj",
    )(x2d, g.reshape(1, d), w)


def _rwkv_prep_body(z_ref, zp_ref, zn_ref, mup_ref, mun_ref, w0_ref, w2_ref, a0_ref, a2_ref, g2_ref,
                    kk_ref, ka_ref, rk_ref, ones_ref, trif_ref, trib_ref,
                    r_o, v_o, kn_o, lwf_o, lwb_o, kf_o, kb_o, bf_o, bb_o, gate_o, bonus_o):
    i = pl.program_id(1)
    n = pl.num_programs(1)
    z = z_ref[...]
    tm = z.shape[0]
    rows = lax.broadcasted_iota(jnp.int32, z.shape, 0)
    prev_row = jnp.where(i == 0, 0.0, zp_ref[HALO - 1:HALO, :])
    next_row = jnp.where(i == n - 1, 0.0, zn_ref[0:1, :])
    z_prev = jnp.where(rows == 0, prev_row, pltpu.roll(z, 1, 0))
    z_next = jnp.where(rows == tm - 1, next_row, pltpu.roll(z, tm - 1, 0))
    zs = z + mup_ref[...] * (z_prev - z) + mun_ref[...] * (z_next - z)

    w = RWKV_WIDTH
    r = zs[:, 0:w]
    k = zs[:, w:2 * w]
    v = zs[:, 2 * w:3 * w]
    wd = zs[:, 3 * w:3 * w + LANES]
    ad = zs[:, 3 * w + LANES:3 * w + 2 * LANES]
    gd = zs[:, 3 * w + 2 * LANES:3 * w + 3 * LANES]

    yw = w0_ref[...] + _dot(jnp.tanh(wd).astype(BF16), w2_ref[...])
    lw = -math.exp(-0.5) * _sigmoid(yw)
    iclr = _sigmoid(a0_ref[...] + _dot(ad.astype(BF16), a2_ref[...]))
    gate = _dot(_sigmoid(gd).astype(BF16), g2_ref[...])

    ones_bd = ones_ref[...]
    kk = k * kk_ref[...]
    kn = kk / jnp.maximum(jnp.sqrt(_segsum(kk * kk, ones_bd)), 1e-12)
    ka = ka_ref[...]
    k_f = k * (1.0 + (iclr[:, :w] - 1.0) * ka)
    k_b = k * (1.0 + (iclr[:, w:] - 1.0) * ka)
    bonus = _segsum(r * rk_ref[...] * (k_f + k_b), ones_bd)

    r_o[...] = r
    v_o[...] = v
    kn_o[...] = kn
    lwf_o[...] = _dot_split3(trif_ref[...], lw[:, :w])
    lwb_o[...] = _dot_split3(trib_ref[...], lw[:, w:])
    kf_o[...] = k_f
    kb_o[...] = k_b
    bf_o[...] = kn * iclr[:, :w]
    bb_o[...] = kn * iclr[:, w:]
    gate_o[...] = gate
    bonus_o[...] = bonus * v


def _rwkv_prep(z, bsz, s_len, consts, tm=RWKV_PREP_ROWS):
    t, n = z.shape
    nb = s_len // tm
    nh = t // HALO
    w = RWKV_WIDTH

    def full(a):
        return pl.BlockSpec(a.shape, lambda b, i: (0,) * a.ndim)

    out = jax.ShapeDtypeStruct((t, w), F32)
    return pl.pallas_call(
        _rwkv_prep_body,
        out_shape=(out,) * 11,
        grid=(bsz, nb),
        in_specs=[pl.BlockSpec((tm, n), lambda b, i: (b * nb + i, 0)),
                  pl.BlockSpec((HALO, n), lambda b, i: (jnp.maximum((b * nb + i) * (tm // HALO) - 1, 0), 0)),
                  pl.BlockSpec((HALO, n), lambda b, i: (jnp.minimum((b * nb + i + 1) * (tm // HALO), nh - 1), 0)),
                  ] + [full(a) for a in consts],
        out_specs=(pl.BlockSpec((tm, w), lambda b, i: (b * nb + i, 0)),) * 11,
        compiler_params=_params("parallel", "parallel"),
        name="rwkv_prep",
    )(z, z, z, *consts)


def _wkv_items(reverse, r_ref, v_ref, kn_ref, cum_ref, k_ref, b_ref, y_ref, ht_ref):
    c = CHUNK
    half = LANES // 2
    row2 = lax.broadcasted_iota(jnp.int32, (2 * c, LANES), 0)
    lane2 = lax.broadcasted_iota(jnp.int32, (2 * c, LANES), 1)
    t_row, t_col = row2 % c, lane2 % c
    own = (row2 // c) == (lane2 // HEAD_DIM)
    first = row2 < c
    order = (t_row < t_col, t_row <= t_col) if reverse else (t_row > t_col, t_row >= t_col)
    strict, incl = own & order[0], own & order[1]
    lane = lax.broadcasted_iota(jnp.int32, (c, LANES), 1)
    m0 = lane < HEAD_DIM

    def stack(x):
        return jnp.concatenate([jnp.where(m0, x, 0.0), jnp.where(m0, 0.0, x)], axis=0)

    trow = lax.broadcasted_iota(jnp.int32, (c, LANES), 0)
    items = []
    subs = range(WKV_CHUNKS_PER_STEP)
    for sub, p in [(s, p) for s in (reversed(subs) if reverse else subs) for p in range(RWKV_WIDTH // LANES)]:
        rows = slice(sub * c, (sub + 1) * c)
        sl = slice(p * LANES, (p + 1) * LANES)
        cum = cum_ref[rows, sl]
        if reverse:
            cum_ex = jnp.where(trow == c - 1, 0.0, pltpu.roll(cum, c - 1, 0))
            tot = cum[0:1]
        else:
            cum_ex = jnp.where(trow == 0, 0.0, pltpu.roll(cum, 1, 0))
            tot = cum[c - 1:c]
        r, v, kn, k, b = r_ref[rows, sl], v_ref[rows, sl], kn_ref[rows, sl], k_ref[rows, sl], b_ref[rows, sl]
        e_neg = jnp.exp(-cum)
        e_end = jnp.exp(tot - cum)
        am2 = stack(-kn * jnp.exp(cum_ex))
        rm2 = stack(r * jnp.exp(cum))
        vm2 = stack(v).astype(BF16)
        rhs = jnp.concatenate([b * e_neg, k * e_neg], axis=0).astype(BF16)
        aa = _dot_nt(jnp.concatenate([am2, rm2], axis=0).astype(BF16), rhs)
        ar = pltpu.roll(aa, half, 1)
        aa_a, ar_a, aa_r, ar_r = aa[:2 * c], ar[:2 * c], aa[2 * c:], ar[2 * c:]
        items.append(dict(
            p=p, rows=rows, sl=sl, y_ref=y_ref, ht_ref=ht_ref, own=own, v=v, am2=am2, rm2=rm2, vm2=vm2,
            etot=jnp.exp(tot),
            bk=jnp.concatenate([b * e_end, k * e_end], axis=0).astype(BF16),
            abd_b=jnp.where(strict, jnp.where(first, aa_a, ar_a), 0.0),
            abd_k=jnp.where(strict, jnp.where(first, ar_a, aa_a), 0.0).astype(BF16),
            rbd=jnp.concatenate([jnp.where(incl, jnp.where(first, aa_r, ar_r), 0.0),
                                 jnp.where(incl, jnp.where(first, ar_r, aa_r), 0.0)], axis=1).astype(BF16)))
    return items


def _wkv_solve(items):
    half = LANES // 2
    for it in items:
        it["x"] = it["am2"] + pltpu.roll(_dot(it["abd_k"], it["vm2"]), half, 1)
        it["pw"] = it["abd_b"]
    n_lvl = int(math.log2(CHUNK))
    for lvl in range(n_lvl):
        for it in items:
            x, pw = it["x"], it["pw"].astype(BF16)
            if lvl < n_lvl - 1:
                pr = _dot(pw, jnp.concatenate([x, it["pw"]], axis=1).astype(BF16))
                it["x"] = x + pr[:, :LANES]
                it["pw"] = pr[:, LANES:]
            else:
                it["x"] = x + _dot(pw, x.astype(BF16))


def _wkv_finish(items):
    c = CHUNK
    half = LANES // 2
    r128 = lax.broadcasted_iota(jnp.int32, (LANES, LANES), 0)
    c128 = lax.broadcasted_iota(jnp.int32, (LANES, LANES), 1)
    same_head = (r128 // HEAD_DIM) == (c128 // HEAD_DIM)
    for it in items:
        x, own, rbd, bk = it["x"], it["own"], it["rbd"], it["bk"]
        p1_2 = jnp.where(own, x, 0.0)
        p2_2 = jnp.where(own, pltpu.roll(x, half, 1), 0.0)
        q1_2 = it["rm2"] + _dot(rbd[:, :LANES], p1_2.astype(BF16))
        q2_2 = _dot(rbd, jnp.concatenate([p2_2.astype(BF16), it["vm2"]], axis=0))
        p1, p2 = p1_2[:c] + p1_2[c:], p2_2[:c] + p2_2[c:]
        q1, q2 = q1_2[:c] + q1_2[c:], q2_2[:c] + q2_2[c:]
        bp = jnp.where(same_head, _dot_tn(bk[:c], p1.astype(BF16)), 0.0)
        gt = jnp.where(same_head, _dot_tn(jnp.concatenate([p2, it["v"]], axis=0).astype(BF16), bk), 0.0)
        ht = it["ht_ref"][it["p"]]
        hb = ht.astype(BF16)
        it["y_ref"][it["rows"], it["sl"]] = _dot_nt(q1.astype(BF16), hb) + q2
        it["ht_ref"][it["p"]] = it["etot"] * ht + _dot_nt(hb, bp.astype(BF16)) + gt


def _wkv_body(rf, vf, knf, cumf, kf, bf, rb, vb, knb, cumb, kb, bb, yf, yb, hf_ref, hb_ref):
    @pl.when(pl.program_id(1) == 0)
    def _():
        hf_ref[...] = jnp.zeros_like(hf_ref)
        hb_ref[...] = jnp.zeros_like(hb_ref)

    items = (_wkv_items(False, rf, vf, knf, cumf, kf, bf, yf, hf_ref)
             + _wkv_items(True, rb, vb, knb, cumb, kb, bb, yb, hb_ref))
    _wkv_solve(items)
    _wkv_finish(items)


def _wkv_scan(r, v, kn, cumf, cumb, kf, kb, bf, bb, bsz, s_len):
    t, w = r.shape
    c = CHUNK * WKV_CHUNKS_PER_STEP
    nc = s_len // c
    fwd = pl.BlockSpec((c, w), lambda b, i: (b * nc + i, 0))
    bwd = pl.BlockSpec((c, w), lambda b, i: (b * nc + nc - 1 - i, 0))
    out = jax.ShapeDtypeStruct((t, w), F32)
    n_pairs = w // LANES
    return pl.pallas_call(
        _wkv_body,
        out_shape=(out, out),
        grid=(bsz, nc),
        in_specs=[fwd] * 6 + [bwd] * 6,
        out_specs=(fwd, bwd),
        scratch_shapes=[pltpu.VMEM((n_pairs, LANES, LANES), F32),
                        pltpu.VMEM((n_pairs, LANES, LANES), F32)],
        compiler_params=_params("parallel", "arbitrary"),
        name="wkv_scan",
    )(r, v, kn, cumf, kf, bf, r, v, kn, cumb, kb, bb)


def _dft_split(s_len):
    s1 = 1 << ((s_len.bit_length() - 1 + 1) // 2)
    return s1, s_len // s1


def _dft_tables(s_len):
    s1, s2 = _dft_split(s_len)

    def cs(num, period):
        ang = (num % period).astype(F32) * (2.0 * math.pi / period)
        return jnp.cos(ang), jnp.sin(ang)

    k1 = jnp.arange(s1, dtype=jnp.int32)
    c1, sn1 = cs(k1[:, None] * k1[None, :], s1)
    w1 = jnp.concatenate([c1, sn1], axis=0)
    k = k1[:, None, None] + s1 * jnp.arange(s2, dtype=jnp.int32)[None, :, None]
    ec, es = cs(k * jnp.arange(s2, dtype=jnp.int32)[None, None, :], s_len)
    e = jnp.concatenate([jnp.concatenate([ec, -es], axis=2),
                         jnp.concatenate([-es, -ec], axis=2)], axis=1)
    c = jnp.arange(HEAD_DIM, dtype=jnp.int32)
    c3, s3 = cs(c[:, None] * c[None, :], HEAD_DIM)
    eye = jnp.eye(2 * LANES // HEAD_DIM, dtype=F32)
    return (w1.astype(BF16), e.astype(BF16), jnp.kron(eye, c3).astype(BF16), jnp.kron(eye, s3).astype(BF16))


def _dft1_body(w_ref, u_ref, y_ref):
    y_ref[...] = _dot(w_ref[...], u_ref[...]).astype(BF16)


def _dft2_body(scale, e_ref, y_ref, c4_ref, s4_ref, f_ref):
    g = e_ref.shape[0]
    s2 = y_ref.shape[2]
    nw = 2 * LANES
    for j in range(g):
        yy = jnp.concatenate([y_ref[0, j], y_ref[1, j]], axis=0)
        x = _dot(e_ref[j], yy)
        xr = x[:s2].astype(BF16)
        xi = x[s2:].astype(BF16)
        for q in range(FNET_WIDTH // nw):
            f = _dot(xr[:, q * nw:(q + 1) * nw], c4_ref[...]) + _dot(xi[:, q * nw:(q + 1) * nw], s4_ref[...])
            f_ref[:, j * FNET_WIDTH + q * nw:j * FNET_WIDTH + (q + 1) * nw] = f * scale


def _fnet(u, bsz, s_len, tables):
    w1, e, c4, s4 = tables
    s1, s2 = _dft_split(s_len)
    fw = FNET_WIDTH
    ncol = s2 * fw
    nb = min(ncol, 4096)
    u3 = u.reshape(bsz, s1, ncol)
    y = pl.pallas_call(
        _dft1_body,
        out_shape=jax.ShapeDtypeStruct((bsz, 2 * s1, ncol), BF16),
        grid=(bsz, ncol // nb),
        in_specs=[pl.BlockSpec((2 * s1, s1), lambda b, j: (0, 0)),
                  pl.BlockSpec((None, s1, nb), lambda b, j: (b, 0, j))],
        out_specs=pl.BlockSpec((None, 2 * s1, nb), lambda b, j: (b, 0, j)),
        compiler_params=_params("parallel", "parallel"),
        name="dft_stage1",
    )(w1, u3)
    y5 = y.reshape(bsz, 2, s1, s2, fw)
    g = 8
    scale = 1.0 / math.sqrt(s_len * HEAD_DIM)
    f = pl.pallas_call(
        functools.partial(_dft2_body, scale),
        out_shape=jax.ShapeDtypeStruct((bsz, s2, s1 * fw), F32),
        grid=(bsz, s1 // g),
        in_specs=[pl.BlockSpec((g, 2 * s2, 2 * s2), lambda b, j: (j, 0, 0)),
                  pl.BlockSpec((None, 2, g, s2, fw), lambda b, j: (b, 0, j, 0, 0)),
                  pl.BlockSpec(c4.shape, lambda b, j: (0, 0)),
                  pl.BlockSpec(s4.shape, lambda b, j: (0, 0))],
        out_specs=pl.BlockSpec((None, s2, g * fw), lambda b, j: (b, 0, j)),
        compiler_params=_params("parallel", "parallel"),
        name="dft_stage2",
    )(e, y5, c4, s4)
    return f.reshape(bsz * s_len, fw)


def _mix0_out_body(x_ref, f_ref, yf_ref, yb_ref, bonus_ref, gate_ref, lng_ref, lnb_ref, ones_ref, w_ref, g_ref,
                   o_ref):
    ones_bd = ones_ref[...]
    y = yf_ref[...] + yb_ref[...]
    mu = _segsum(y, ones_bd) * (1.0 / HEAD_DIM)
    d = y - mu
    var = _segsum(d * d, ones_bd) * (1.0 / HEAD_DIM)
    yn = d * lax.rsqrt(var + GN_EPS) * lng_ref[...] + lnb_ref[...]
    o = (yn + bonus_ref[...]) * gate_ref[...]
    cat = jnp.concatenate([f_ref[...].astype(BF16), o.astype(BF16)], axis=1)
    h = _dot(cat, w_ref[...])
    o_ref[...] = x_ref[...] + _rms(h, g_ref[...], NORM_EPS)


def _mix0_out(x2d, f, yf, yb, bonus, gate, lng, lnb, ones_bd, w_out, g_post, tm=512):
    t, d = x2d.shape
    w = RWKV_WIDTH
    row = lambda n: pl.BlockSpec((tm, n), lambda i: (i, 0))
    cst = lambda a: pl.BlockSpec(a.shape, lambda i: (0, 0))
    return pl.pallas_call(
        _mix0_out_body,
        out_shape=jax.ShapeDtypeStruct((t, d), F32),
        grid=(t // tm,),
        in_specs=[row(d), row(w), row(w), row(w), row(w), row(w),
                  cst(lng), cst(lnb), cst(ones_bd), cst(w_out), cst(g_post)],
        out_specs=row(d),
        compiler_params=_params("parallel"),
        name="mix0_out",
    )(x2d, f, yf, yb, bonus, gate, lng, lnb, ones_bd, w_out, g_post)


FF_CHUNK = 256


def _ffn_body(x_ref, xp_ref, xn_ref, gpre_ref, wup_ref, cw_ref, cb_ref, wdn_ref, gpost_ref, o_ref,
              h_ref, acc_ref, ua_ref, ub_ref):
    i = pl.program_id(1)
    n = pl.num_programs(1)
    tm = x_ref.shape[0]
    f = FF_CHUNK
    g = gpre_ref[...]
    x = x_ref[...]
    hp = jnp.where(i == 0, 0.0, _rms(xp_ref[...], g, NORM_EPS))
    hn = jnp.where(i == n - 1, 0.0, _rms(xn_ref[...], g, NORM_EPS))
    h_ref[...] = jnp.concatenate([hp, _rms(x, g, NORM_EPS), hn], axis=0).astype(BF16)
    acc_ref[...] = jnp.zeros_like(acc_ref)

    def cols(j):
        return j * f, D_FF + j * f

    def up(j, dst):
        h = h_ref[...]
        for part, c0 in enumerate(cols(j)):
            dst[:, part * f:(part + 1) * f] = _dot(h, wup_ref[:, pl.ds(c0, f)])

    def post(j, src):
        def conv(part, c0):
            u = src[:, part * f:(part + 1) * f]
            cw = cw_ref[:, pl.ds(c0, f)]
            return (cw[0:1] * u[HALO - 1:HALO - 1 + tm] + cw[1:2] * u[HALO:HALO + tm]
                    + cw[2:3] * u[HALO + 1:HALO + 1 + tm] + cb_ref[:, pl.ds(c0, f)])

        cv, cg = cols(j)
        val = conv(0, cv)
        gate = conv(1, cg)
        act = 0.5 * gate * (1.0 + jnp.tanh(math.sqrt(2.0 / math.pi) * (gate + 0.044715 * gate * gate * gate)))
        acc_ref[...] += _dot((act * val).astype(BF16), wdn_ref[pl.ds(cv, f), :])

    n_chunks = D_FF // f
    bufs = (ua_ref, ub_ref)
    up(0, ua_ref)
    for j in range(n_chunks):
        if j + 1 < n_chunks:
            up(j + 1, bufs[(j + 1) % 2])
        post(j, bufs[j % 2])
    o_ref[...] = x + _rms(acc_ref[...], gpost_ref[...], NORM_EPS)


def _ffn(x2d, bsz, s_len, g_pre, w_up, conv_w, conv_b, w_down, g_post, tm=512):
    t, d = x2d.shape
    nb = s_len // tm
    nh = t // HALO
    cst = lambda a: pl.BlockSpec(a.shape, lambda b, i: (0,) * a.ndim, pipeline_mode=pl.Buffered(1))
    return pl.pallas_call(
        _ffn_body,
        out_shape=jax.ShapeDtypeStruct((t, d), F32),
        grid=(bsz, nb),
        in_specs=[pl.BlockSpec((tm, d), lambda b, i: (b * nb + i, 0)),
                  pl.BlockSpec((HALO, d), lambda b, i: (jnp.maximum((b * nb + i) * (tm // HALO) - 1, 0), 0)),
                  pl.BlockSpec((HALO, d), lambda b, i: (jnp.minimum((b * nb + i + 1) * (tm // HALO), nh - 1), 0)),
                  cst(g_pre), cst(w_up), cst(conv_w), cst(conv_b), cst(w_down), cst(g_post)],
        out_specs=pl.BlockSpec((tm, d), lambda b, i: (b * nb + i, 0)),
        scratch_shapes=[pltpu.VMEM((tm + 2 * HALO, d), BF16), pltpu.VMEM((tm, d), F32),
                        pltpu.VMEM((tm + 2 * HALO, 2 * FF_CHUNK), F32),
                        pltpu.VMEM((tm + 2 * HALO, 2 * FF_CHUNK), F32)],
        compiler_params=_params("parallel", "parallel"),
        name="conv_ffn",
    )(x2d, x2d, x2d, g_pre, w_up, conv_w, conv_b, w_down, g_post)


def _qkv_body(x_ref, g_ref, wk_ref, wqt_ref, wvt_ref, ck_ref, sk_ref, cq_ref, sq_ref, k_o, qt_o, vt_o):
    h = _rms(x_ref[...], g_ref[...], NORM_EPS).astype(BF16)
    ck, sk = ck_ref[...], sk_ref[...]
    cq, sq = cq_ref[...], sq_ref[...]
    half = LANES // 2
    for hp in range(DIFF_HEADS // 2):
        kp = _dot(h, wk_ref[:, hp * 2 * LANES:(hp + 1) * 2 * LANES])
        for e in range(2):
            kh = kp[:, e * LANES:(e + 1) * LANES]
            sl = slice((2 * hp + e) * LANES, (2 * hp + e + 1) * LANES)
            k_o[:, sl] = (kh * ck + pltpu.roll(kh, half, 1) * sk).astype(BF16)
    qt_all = _dot_nt(wqt_ref[...], h)
    vt_o[...] = _dot_nt(wvt_ref[...], h).astype(BF16)
    for hd in range(DIFF_HEADS):
        sl = slice(hd * LANES, (hd + 1) * LANES)
        qh = qt_all[sl]
        qsw = jnp.concatenate([qh[half:], qh[:half]], axis=0)
        qt_o[sl, :] = ((qh * cq + qsw * sq) * (HEAD_DIM ** -0.5 * math.log2(math.e))).astype(BF16)


def _qkv(x2d, bsz, s_len, g, wk, wqt, wvt, tabs, tm=512):
    t, d = x2d.shape
    nb = s_len // tm
    ck, sk, cq, sq = tabs
    cst = lambda a: pl.BlockSpec(a.shape, lambda b, i: (0, 0))
    tshape = jax.ShapeDtypeStruct((bsz, d, s_len), BF16)
    return pl.pallas_call(
        _qkv_body,
        out_shape=(jax.ShapeDtypeStruct((t, d), BF16), tshape, tshape),
        grid=(bsz, nb),
        in_specs=[pl.BlockSpec((tm, d), lambda b, i: (b * nb + i, 0)), cst(g), cst(wk), cst(wqt), cst(wvt),
                  pl.BlockSpec((tm, LANES), lambda b, i: (i, 0)),
                  pl.BlockSpec((tm, LANES), lambda b, i: (i, 0)),
                  pl.BlockSpec((LANES, tm), lambda b, i: (0, i)),
                  pl.BlockSpec((LANES, tm), lambda b, i: (0, i))],
        out_specs=(pl.BlockSpec((tm, d), lambda b, i: (b * nb + i, 0)),
                   pl.BlockSpec((None, d, tm), lambda b, i: (b, 0, i)),
                   pl.BlockSpec((None, d, tm), lambda b, i: (b, 0, i))),
        compiler_params=_params("parallel", "parallel"),
        name="qkv_rope",
    )(x2d, g, wk, wqt, wvt, ck, sk, cq, sq)


NEG_BIG = -1e30
ATTN_ONES_ROWS = 16


def _attn_split_q(qt):
    rowi = lax.broadcasted_iota(jnp.int32, qt.shape, 0)
    first = (rowi // (HEAD_DIM // 2)) % 2 == 0
    zero = jnp.zeros_like(qt)
    return jnp.where(first, qt, zero), jnp.where(first, zero, qt)


def _attn_finish(lambda_init, acc0, acc1, lq1, lk1, lq2, lk2, sg_ref, o_ref):
    lam = (jnp.exp(jnp.sum(lq1[...] * lk1[...], axis=1, keepdims=True))
           - jnp.exp(jnp.sum(lq2[...] * lk2[...], axis=1, keepdims=True)) + lambda_init)
    o = acc0[:LANES] / acc0[LANES:LANES + 1] - lam * (acc1[:LANES] / acc1[LANES:LANES + 1])
    o = o * lax.rsqrt(jnp.mean(o * o, axis=0, keepdims=True) + SUBLN_EPS) * sg_ref[...] * (1.0 - lambda_init)
    o_ref[...] = o.astype(BF16)


def _attn_body(lambda_init, tk, qt_ref, k_ref, vt_ref, lq1, lk1, lq2, lk2, sg_ref, o_ref, sa_ref, sb_ref, acc_ref):
    qh = _attn_split_q(qt_ref[...])
    tq = qt_ref.shape[1]
    s_len = k_ref.shape[0]

    def scores(j, dst):
        kk = k_ref[pl.ds(pl.multiple_of(j * tk, tk), tk), :]
        for c in range(2):
            dst[c] = _dot(kk, qh[c])

    ones_row = (lax.broadcasted_iota(jnp.int32, (ATTN_ONES_ROWS, tk), 0) == 0).astype(BF16)

    def update(j, src, ms):
        vt = vt_ref[:, pl.ds(pl.multiple_of(j * tk, tk), tk)]
        vte = jnp.concatenate([vt, ones_row], axis=0)
        out = []
        for c in range(2):
            s = src[c]
            mn = jnp.maximum(ms[c], jnp.max(s, axis=0, keepdims=True))
            p = jnp.exp2(s - mn).astype(BF16)
            acc_ref[c] = jnp.exp2(ms[c] - mn) * acc_ref[c] + _dot(vte, p)
            out.append(mn)
        return tuple(out)

    mi = jnp.full((1, tq), NEG_BIG, F32)
    acc_ref[...] = jnp.zeros_like(acc_ref)
    n_tiles = s_len // tk

    def step(i, ms):
        scores(2 * i + 1, sb_ref)
        ms = update(2 * i, sa_ref, ms)
        scores(2 * i + 2, sa_ref)
        return update(2 * i + 1, sb_ref, ms)

    scores(0, sa_ref)
    trips = n_tiles // 2 - 1
    ms = lax.fori_loop(0, trips, step, (mi, mi), unroll=2 if trips >= 4 else 1)
    scores(n_tiles - 1, sb_ref)
    ms = update(n_tiles - 2, sa_ref, ms)
    update(n_tiles - 1, sb_ref, ms)
    _attn_finish(lambda_init, acc_ref[0], acc_ref[1], lq1, lk1, lq2, lk2, sg_ref, o_ref)


def _attention(qt, k, vt, lq1, lk1, lq2, lk2, sg, lambda_init, bsz, s_len, tq=256, tk=512):
    d = qt.shape[1]
    assert s_len % (2 * tk) == 0 and s_len % tq == 0, (s_len, tq, tk)
    k3 = k.reshape(bsz, s_len, d)
    cst = lambda a: pl.BlockSpec(a.shape, lambda b, h, i: (0, 0))
    scratch = [pltpu.VMEM((2, tk, tq), F32), pltpu.VMEM((2, tk, tq), F32),
               pltpu.VMEM((2, LANES + ATTN_ONES_ROWS, tq), F32)]
    return pl.pallas_call(
        functools.partial(_attn_body, lambda_init, tk),
        out_shape=jax.ShapeDtypeStruct((bsz, d, s_len), BF16),
        grid=(bsz, DIFF_HEADS, s_len // tq),
        in_specs=[pl.BlockSpec((None, LANES, tq), lambda b, h, i: (b, h, i)),
                  pl.BlockSpec((None, s_len, LANES), lambda b, h, i: (b, 0, h)),
                  pl.BlockSpec((None, LANES, s_len), lambda b, h, i: (b, h, 0)),
                  cst(lq1), cst(lk1), cst(lq2), cst(lk2), cst(sg)],
        out_specs=pl.BlockSpec((None, LANES, tq), lambda b, h, i: (b, h, i)),
        scratch_shapes=scratch,
        compiler_params=_params("parallel", "parallel", "parallel"),
        name="diff_attn",
    )(qt, k3, vt, lq1, lk1, lq2, lk2, sg)


def _attn_out_body(x_ref, ot_ref, w_ref, g_ref, o_ref):
    h = _dot_tn(ot_ref[...], w_ref[...])
    o_ref[...] = x_ref[...] + _rms(h, g_ref[...], NORM_EPS)


def _attn_out(x2d, ot, w_o, g_post, bsz, s_len, tm=512):
    t, d = x2d.shape
    nb = s_len // tm
    cst = lambda a: pl.BlockSpec(a.shape, lambda b, i: (0, 0))
    return pl.pallas_call(
        _attn_out_body,
        out_shape=jax.ShapeDtypeStruct((t, d), F32),
        grid=(bsz, nb),
        in_specs=[pl.BlockSpec((tm, d), lambda b, i: (b * nb + i, 0)),
                  pl.BlockSpec((None, d, tm), lambda b, i: (b, 0, i)), cst(w_o), cst(g_post)],
        out_specs=pl.BlockSpec((tm, d), lambda b, i: (b * nb + i, 0)),
        compiler_params=_params("parallel", "parallel"),
        name="attn_out",
    )(x2d, ot, w_o, g_post)


def _block_diag2(a):
    z = jnp.zeros_like(a[0])
    return jnp.concatenate([jnp.concatenate([a[0], z], axis=1), jnp.concatenate([z, a[1]], axis=1)], axis=0)


def _rope_perm():
    half = HEAD_DIM // 2
    idx = []
    for h in range(DIFF_HEADS):
        for part in range(2):
            for c in range(2):
                idx.extend(h * 2 * HEAD_DIM + c * HEAD_DIM + part * half + np.arange(half))
    return np.asarray(idx)


def _rope_tables(s_len):
    half = HEAD_DIM // 2
    inv_freq = ROPE_THETA ** (-jnp.arange(half, dtype=F32) / half)
    ang = jnp.arange(s_len, dtype=F32)[:, None] * inv_freq[None, :]
    cos, sin = jnp.cos(ang), jnp.sin(ang)
    ck = jnp.concatenate([cos] * 4, axis=1)
    sk = jnp.concatenate([-sin, -sin, sin, sin], axis=1)
    return ck, sk, ck.T, sk.T


def kernel(x_prompt, x_sample, mix0_norm_pre, mix0_norm_post, w_in0, mu_prev, mu_next, decay_w0, decay_w2, iclr_a0, iclr_a2, gate_g2, k_k, k_a, r_k, lnx_g, lnx_b, w_out0, mix1_norm_pre, mix1_norm_post, w_qkv1, lambda_q1, lambda_k1, lambda_q2, lambda_k2, subln_g, w_o1, ffn_norm_pre, ffn_norm_post, w_up, conv_w, conv_b, w_down):
    w = RWKV_WIDTH
    ones_bd = jnp.asarray(np.kron(np.eye(w // HEAD_DIM), np.ones((HEAD_DIM, HEAD_DIM))), F32).astype(BF16)
    row = lambda a: a.reshape(1, -1)

    w_in_bf = w_in0[0].astype(BF16)
    tri_bd = np.kron(np.eye(RWKV_PREP_ROWS // CHUNK, dtype=np.float32), np.tril(np.ones((CHUNK, CHUNK), np.float32)))
    prep_consts = (row(mu_prev[0]), row(mu_next[0]),
                   row(decay_w0[0]), _block_diag2(decay_w2[0]).astype(BF16),
                   row(iclr_a0[0]), _block_diag2(iclr_a2[0]).astype(BF16),
                   gate_g2[0].astype(BF16), row(k_k[0]), row(k_a[0]), row(r_k[0]), ones_bd,
                   jnp.asarray(tri_bd).astype(BF16), jnp.asarray(tri_bd.T).astype(BF16))
    w_out_bf = w_out0[0].astype(BF16)

    perm = _rope_perm()
    dq = DIFF_HEADS * 2 * HEAD_DIM
    wq = w_qkv1[0][:, :dq][:, perm]
    wk = w_qkv1[0][:, dq:2 * dq][:, perm]
    wv = w_qkv1[0][:, 2 * dq:]
    wk_bf, wqt_bf, wvt_bf = wk.astype(BF16), wq.T.astype(BF16), wv.T.astype(BF16)
    w_o_bf = w_o1[0].astype(BF16)
    lambda_init = 0.8 - 0.6 * math.exp(-0.3 * 1)
    sg_col = subln_g[0].reshape(-1, 1)

    w_up_bf = w_up.astype(BF16)
    w_down_bf = w_down.astype(BF16)

    def ffn(x2d, bsz, s_len, layer):
        return _ffn(x2d, bsz, s_len, row(ffn_norm_pre[layer]), w_up_bf[layer], conv_w[layer],
                    row(conv_b[layer]), w_down_bf[layer], row(ffn_norm_post[layer]))

    def trunk(x):
        bsz, s_len, d = x.shape
        x2d = x.reshape(bsz * s_len, d)
        u, z = _in_proj(x2d, mix0_norm_pre[0], w_in_bf)
        f = _fnet(u, bsz, s_len, _dft_tables(s_len))
        r, v, kn, cumf, cumb, kf, kb, bf, bb, gate, bonus = _rwkv_prep(z, bsz, s_len, prep_consts)
        yf, yb = _wkv_scan(r, v, kn, cumf, cumb, kf, kb, bf, bb, bsz, s_len)
        x2d = _mix0_out(x2d, f, yf, yb, bonus, gate, row(lnx_g[0]), row(lnx_b[0]), ones_bd, w_out_bf,
                        row(mix0_norm_post[0]))
        x2d = ffn(x2d, bsz, s_len, 0)
        k, qt, vt = _qkv(x2d, bsz, s_len, row(mix1_norm_pre[0]), wk_bf, wqt_bf, wvt_bf, _rope_tables(s_len))
        ot = _attention(qt, k, vt, row(lambda_q1[0]), row(lambda_k1[0]), row(lambda_q2[0]), row(lambda_k2[0]),
                        sg_col, lambda_init, bsz, s_len)
        x2d = _attn_out(x2d, ot, w_o_bf, row(mix1_norm_post[0]), bsz, s_len)
        x2d = ffn(x2d, bsz, s_len, 1)
        return x2d.reshape(bsz, s_len, d)

    return (trunk(x_prompt), trunk(x_sample))
```
